```python
import math
import jax, jax.numpy as jnp
from jax import lax
import numpy as np

D_MODEL = 4096
BATCH = 1
SEQ = 16384
DEPTH = 4

SSD_WIDTH = D_MODEL // 2
SSD_HEAD_DIM = 64
SSD_HEADS = SSD_WIDTH // SSD_HEAD_DIM
SSD_GROUPS = 8
SSD_STATE = 128
SSD_CONV = 4
SSD_CHUNK = 128
XBC_WIDTH = SSD_WIDTH + 2 * SSD_GROUPS * SSD_STATE
ATT_WIDTH = D_MODEL // 2
ATT_HEAD_DIM = 128
ATT_HEADS = ATT_WIDTH // ATT_HEAD_DIM
ATT_BLOCK = 128
EVEN_IN_WIDTH = SSD_WIDTH + XBC_WIDTH + SSD_HEADS + 3 * ATT_WIDTH + ATT_HEADS
SG_WIDTH = D_MODEL
SG_GROUPS = 16
SG_CHUNK = 128
FFN_HIDDEN = 4 * D_MODEL
NORM_EPS = 1e-6
N_EVEN = (DEPTH + 1) // 2
N_ODD = DEPTH // 2

kernel_name = "hybrid_ssd_fox_chunked_gmlp_trunk"


def _split_points(widths):
    pts, acc = [], 0
    for w in widths[:-1]:
        acc += w
        pts.append(acc)
    return pts


def rms_norm(x, gain, groups=1):
    xf = x.astype(jnp.float32)
    shp = xf.shape
    xg = xf.reshape(shp[:-1] + (groups, shp[-1] // groups))
    xg = xg * lax.rsqrt(jnp.mean(xg * xg, axis=-1, keepdims=True) + NORM_EPS)
    return (xg.reshape(shp) * gain.astype(jnp.float32)).astype(x.dtype)


def layer_norm(x, gain, bias):
    xf = x.astype(jnp.float32)
    mu = jnp.mean(xf, axis=-1, keepdims=True)
    xc = xf - mu
    var = jnp.mean(xc * xc, axis=-1, keepdims=True)
    y = xc * lax.rsqrt(var + NORM_EPS) * gain.astype(jnp.float32) + bias.astype(jnp.float32)
    return y.astype(x.dtype)


def causal_dwconv(x, w, b):
    y = lax.conv_general_dilated(
        x, w[:, None, :].astype(x.dtype), window_strides=(1,),
        padding=[(SSD_CONV - 1, 0)], dimension_numbers=('NWC', 'WIO', 'NWC'),
        feature_group_count=x.shape[-1])
    return y + b.astype(x.dtype)


def ssd_scan(xh, a, bm, cm):
    Bsz, S, H, P = xh.shape
    G, N = bm.shape[-2:]
    R = H // G
    L = SSD_CHUNK
    nc = S // L
    x = xh.reshape(Bsz, nc, L, G, R, P)
    a = a.reshape(Bsz, nc, L, G, R).transpose(0, 3, 4, 1, 2)
    bm = bm.reshape(Bsz, nc, L, G, N)
    cm = cm.reshape(Bsz, nc, L, G, N)
    a_cs = jnp.cumsum(a, axis=-1)
    idx = jnp.arange(L)
    causal = idx[:, None] >= idx[None, :]
    seg = a_cs[..., :, None] - a_cs[..., None, :]
    decay = jnp.exp(jnp.where(causal, seg, -jnp.inf))
    cb = jnp.einsum('bclgn,bcsgn->bgcls', cm, bm)
    scores = cb[:, :, None] * decay
    y_diag = jnp.einsum('bgrcls,bcsgrp->bclgrp', scores, x)
    decay_states = jnp.exp(a_cs[..., -1:] - a_cs).transpose(0, 3, 4, 1, 2)
    states = jnp.einsum('bclgn,bclgrp->bcgrpn', bm, x * decay_states[..., None])
    chunk_decay = jnp.exp(a_cs[..., -1])

    def step(h, inp):
        dec, st = inp
        return h * dec[..., None, None] + st, h

    h0 = jnp.zeros(states.shape[:1] + states.shape[2:], states.dtype)
    _, prev = lax.scan(step, h0, (jnp.moveaxis(chunk_decay, -1, 0), jnp.moveaxis(states, 1, 0)))
    prev = jnp.moveaxis(prev, 0, 1)
    state_decay_out = jnp.exp(a_cs).transpose(0, 3, 4, 1, 2)
    y_off = jnp.einsum('bclgn,bcgrpn->bclgrp', cm, prev) * state_decay_out[..., None]
    return (y_diag + y_off).reshape(Bsz, S, H, P)


def ssd_mixer(z, xbc, dt_raw, conv_w, conv_b, dt_bias, a_log, d_skip, norm_gain):
    Bsz, S, _ = z.shape
    xbc = jax.nn.silu(causal_dwconv(xbc, conv_w, conv_b))
    xs, bm, cm = jnp.split(xbc, [SSD_WIDTH, SSD_WIDTH + SSD_GROUPS * SSD_STATE], axis=-1)
    xs = xs.reshape(Bsz, S, SSD_HEADS, SSD_HEAD_DIM).astype(jnp.float32)
    bm = bm.reshape(Bsz, S, SSD_GROUPS, SSD_STATE).astype(jnp.float32)
    cm = cm.reshape(Bsz, S, SSD_GROUPS, SSD_STATE).astype(jnp.float32)
    dt = jax.nn.softplus(dt_raw.astype(jnp.float32) + dt_bias.astype(jnp.float32))
    a = -jnp.exp(a_log.astype(jnp.float32)) * dt
    y = ssd_scan(xs * dt[..., None], a, bm, cm)
    y = y + d_skip.astype(jnp.float32)[:, None] * xs
    y = y.reshape(Bsz, S, SSD_WIDTH) * jax.nn.silu(z.astype(jnp.float32))
    return rms_norm(y, norm_gain, groups=SSD_GROUPS).astype(z.dtype)


def forgetting_attention(q, k, v, f_logit, f_bias):
    Bsz, S = q.shape[:2]
    log_f = jax.nn.log_sigmoid(f_logit.astype(jnp.float32) + f_bias.astype(jnp.float32))
    c = jnp.cumsum(log_f, axis=1).transpose(0, 2, 1)
    scale = 1.0 / math.sqrt(ATT_HEAD_DIM)
    q = q.transpose(0, 2, 1, 3) * scale
    k = k.transpose(0, 2, 1, 3)
    v = v.transpose(0, 2, 1, 3)
    outs = []
    for i in range(S // ATT_BLOCK):
        q0, q1 = i * ATT_BLOCK, (i + 1) * ATT_BLOCK
        s = jnp.einsum('bhqd,bhkd->bhqk', q[:, :, q0:q1], k[:, :, :q1]).astype(jnp.float32)
        s = s + c[:, :, q0:q1, None] - c[:, :, None, :q1]
        qi = jnp.arange(q0, q1)
        ki = jnp.arange(q1)
        s = jnp.where(ki[None, :] <= qi[:, None], s, -jnp.inf)
        p = jax.nn.softmax(s, axis=-1).astype(v.dtype)
        outs.append(jnp.einsum('bhqk,bhkd->bhqd', p, v[:, :, :q1]))
    o = jnp.concatenate(outs, axis=2)
    return o.transpose(0, 2, 1, 3).reshape(Bsz, S, ATT_WIDTH)


def even_mixer(xn, w_in, w_out, conv_w, conv_b, dt_bias, a_log, d_skip, norm_gain, f_bias):
    Bsz, S, _ = xn.shape
    h = xn @ w_in
    widths = [SSD_WIDTH, XBC_WIDTH, SSD_HEADS, ATT_WIDTH, ATT_WIDTH, ATT_WIDTH, ATT_HEADS]
    z, xbc, dt_raw, q, k, v, f_logit = jnp.split(h, _split_points(widths), axis=-1)
    y_ssd = ssd_mixer(z, xbc, dt_raw, conv_w, conv_b, dt_bias, a_log, d_skip, norm_gain)
    hs = (Bsz, S, ATT_HEADS, ATT_HEAD_DIM)
    y_att = forgetting_attention(q.reshape(hs), k.reshape(hs), v.reshape(hs), f_logit, f_bias)
    return jnp.concatenate([y_ssd, y_att], axis=-1) @ w_out


def odd_mixer(xn, w_in, w_out, ln_gain, ln_bias, w_s, b_s):
    Bsz, S, _ = xn.shape
    h = jax.nn.gelu(xn @ w_in)
    u, v = jnp.split(h, 2, axis=-1)
    v = layer_norm(v, ln_gain, ln_bias)
    v = v.reshape(Bsz, S // SG_CHUNK, SG_CHUNK, SG_GROUPS, SG_WIDTH // SG_GROUPS)
    mask = jnp.tril(jnp.ones((SG_CHUNK, SG_CHUNK), dtype=bool))
    w = jnp.where(mask, w_s, jnp.zeros_like(w_s))
    vm = jnp.einsum('gts,bcsgd->bctgd', w, v) + b_s.T[None, None, :, :, None]
    return (u * vm.reshape(Bsz, S, SG_WIDTH)) @ w_out


def setup_inputs(seed: int = 0) -> dict:
    key = jax.random.key(seed)
    ks = jax.random.split(key, 24)
    nrm = jax.random.normal
    f32 = jnp.float32
    dt = jnp.exp(jax.random.uniform(ks[9], (N_EVEN, SSD_HEADS), f32, math.log(1e-3), math.log(1e-1)))
    return {
        "x": nrm(ks[0], (BATCH, SEQ, D_MODEL), f32),
        "pre_mix_gain": 1.0 + 0.05 * nrm(ks[1], (DEPTH, D_MODEL), f32),
        "post_mix_gain": 1.0 + 0.05 * nrm(ks[2], (DEPTH, D_MODEL), f32),
        "pre_ffn_gain": 1.0 + 0.05 * nrm(ks[3], (DEPTH, D_MODEL), f32),
        "post_ffn_gain": 1.0 + 0.05 * nrm(ks[4], (DEPTH, D_MODEL), f32),
        "even_w_in": nrm(ks[5], (N_EVEN, D_MODEL, EVEN_IN_WIDTH), f32) * D_MODEL ** -0.5,
        "even_w_out": nrm(ks[6], (N_EVEN, SSD_WIDTH + ATT_WIDTH, D_MODEL), f32) * (SSD_WIDTH + ATT_WIDTH) ** -0.5,
        "ssd_conv_w": nrm(ks[7], (N_EVEN, SSD_CONV, XBC_WIDTH), f32) * SSD_CONV ** -0.5,
        "ssd_conv_b": 0.02 * nrm(ks[8], (N_EVEN, XBC_WIDTH), f32),
        "ssd_dt_bias": dt + jnp.log(-jnp.expm1(-dt)),
        "ssd_a_log": jnp.log(jax.random.uniform(ks[10], (N_EVEN, SSD_HEADS), f32, 1.0, 16.0)),
        "ssd_d": 1.0 + 0.1 * nrm(ks[11], (N_EVEN, SSD_HEADS), f32),
        "ssd_norm_gain": 1.0 + 0.05 * nrm(ks[12], (N_EVEN, SSD_WIDTH), f32),
        "fox_f_bias": jax.random.uniform(ks[13], (N_EVEN, ATT_HEADS), f32, 2.0, 6.0),
        "odd_w_in": nrm(ks[14], (N_ODD, D_MODEL, 2 * SG_WIDTH), f32) * D_MODEL ** -0.5,
        "odd_w_out": nrm(ks[15], (N_ODD, SG_WIDTH, D_MODEL), f32) * SG_WIDTH ** -0.5,
        "sg_ln_gain": 1.0 + 0.05 * nrm(ks[16], (N_ODD, SG_WIDTH), f32),
        "sg_ln_bias": 0.02 * nrm(ks[17], (N_ODD, SG_WIDTH), f32),
        "sg_w_s": nrm(ks[18], (N_ODD, SG_GROUPS, SG_CHUNK, SG_CHUNK), f32) * SG_CHUNK ** -0.5,
        "sg_b_s": 1.0 + 0.1 * nrm(ks[19], (N_ODD, SG_GROUPS, SG_CHUNK), f32),
        "ffn_w_up": nrm(ks[20], (DEPTH, D_MODEL, FFN_HIDDEN), f32) * D_MODEL ** -0.5,
        "ffn_w_down": nrm(ks[21], (DEPTH, FFN_HIDDEN, D_MODEL), f32) * FFN_HIDDEN ** -0.5,
    }


def reference(x, pre_mix_gain, post_mix_gain, pre_ffn_gain, post_ffn_gain,
              even_w_in, even_w_out, ssd_conv_w, ssd_conv_b, ssd_dt_bias, ssd_a_log, ssd_d,
              ssd_norm_gain, fox_f_bias, odd_w_in, odd_w_out, sg_ln_gain, sg_ln_bias,
              sg_w_s, sg_b_s, ffn_w_up, ffn_w_down):
    h = x
    for layer in range(DEPTH):
        i = layer // 2
        xn = rms_norm(h, pre_mix_gain[layer])
        if layer % 2 == 0:
            y = even_mixer(xn, even_w_in[i], even_w_out[i], ssd_conv_w[i], ssd_conv_b[i],
                           ssd_dt_bias[i], ssd_a_log[i], ssd_d[i], ssd_norm_gain[i], fox_f_bias[i])
        else:
            y = odd_mixer(xn, odd_w_in[i], odd_w_out[i], sg_ln_gain[i], sg_ln_bias[i],
                          sg_w_s[i], sg_b_s[i])
        h = h + rms_norm(y, post_mix_gain[layer])
        xn = rms_norm(h, pre_ffn_gain[layer])
        y = jnp.square(jax.nn.relu(xn @ ffn_w_up[layer])) @ ffn_w_down[layer]
        h = h + rms_norm(y, post_ffn_gain[layer])
    return h
```

```python
import functools
import math

import jax
import jax.numpy as jnp
import numpy as np
from jax import lax
from jax.experimental import pallas as pl
from jax.experimental.pallas import tpu as pltpu

F32 = jnp.float32
BF16 = jnp.bfloat16

D_MODEL = 4096
SEQ = 16384
DEPTH = 4
SSD_WIDTH = 2048
SSD_HEAD_DIM = 64
SSD_HEADS = 32
SSD_GROUPS = 8
SSD_HEADS_PER_GROUP = SSD_HEADS // SSD_GROUPS
SSD_GROUP_WIDTH = SSD_WIDTH // SSD_GROUPS
SSD_STATE = 128
SSD_CONV = 4
SSD_CHUNK = 128
XBC_WIDTH = SSD_WIDTH + 2 * SSD_GROUPS * SSD_STATE
ATT_WIDTH = 2048
ATT_HEAD_DIM = 128
ATT_HEADS = 16
SG_WIDTH = 4096
SG_GROUPS = 16
SG_GROUP_WIDTH = SG_WIDTH // SG_GROUPS
SG_CHUNK = 128
FFN_HIDDEN = 4 * D_MODEL
NORM_EPS = 1e-6

V7X_LANES = 128
V7X_SUBLANES = 8
V7X_VMEM_BYTES = 64 * 1024 * 1024
VMEM_LIMIT = V7X_VMEM_BYTES - 8 * 1024 * 1024

COL_Z = 0
COL_XS = SSD_WIDTH
COL_B = COL_XS + SSD_WIDTH
COL_C = COL_B + SSD_GROUPS * SSD_STATE
COL_Q = COL_C + SSD_GROUPS * SSD_STATE
COL_K = COL_Q + ATT_WIDTH
COL_V = COL_K + ATT_WIDTH
EVEN_MAIN_WIDTH = COL_V + ATT_WIDTH
SIDE_WIDTH = (SSD_GROUPS + 1) * V7X_LANES

MM_TM = 1024
MM_TN = 1024
MM_TK = 4096
NORM_ROWS = 128
SSD_ROWS = 512
FOX_T = 1024
FOX_PREP_ROWS = 2048
SG_ROWS = 256


def _params(*sem):
    return pltpu.CompilerParams(dimension_semantics=sem, vmem_limit_bytes=VMEM_LIMIT)


def _activation(y, act):
    if act == "relu2":
        r = jnp.maximum(y, 0.0)
        return r * r
    if act == "gelu_tanh":
        c = math.sqrt(2.0 / math.pi)
        return 0.5 * y * (1.0 + jnp.tanh(c * (y + 0.044715 * (y * y * y))))
    assert act is None
    return y


def _mm_kernel(a_ref, b_ref, o_ref, *, act, nk):
    part = jnp.dot(a_ref[...], b_ref[...], preferred_element_type=F32)
    if nk == 1:
        o_ref[...] = _activation(part, act).astype(o_ref.dtype)
    else:
        k = pl.program_id(2)

        @pl.when(k == 0)
        def _():
            o_ref[...] = part

        @pl.when(k > 0)
        def _():
            o_ref[...] += part


def matmul(a, b, *, act=None, out_dtype=BF16, tm=MM_TM, tn=MM_TN, tk=MM_TK, name):
    m, k = a.shape
    _, n = b.shape
    tn = min(tn, n)
    tk = min(tk, k)
    nk = k // tk
    assert m % tm == 0 and n % tn == 0 and k % tk == 0
    if nk > 1:
        assert act is None and out_dtype == F32
    return pl.pallas_call(
        functools.partial(_mm_kernel, act=act, nk=nk),
        out_shape=jax.ShapeDtypeStruct((m, n), out_dtype),
        grid=(m // tm, n // tn, nk),
        in_specs=[pl.BlockSpec((tm, tk), lambda i, j, kk: (i, kk)),
                  pl.BlockSpec((tk, tn), lambda i, j, kk: (kk, j))],
        out_specs=pl.BlockSpec((tm, tn), lambda i, j, kk: (i, j)),
        compiler_params=_params("parallel", "parallel", "arbitrary"),
        name=name,
    )(a, b)


def _mm2_kernel(a1_ref, a2_ref, b1_ref, b2_ref, o_ref):
    acc = jnp.dot(a1_ref[...], b1_ref[...], preferred_element_type=F32)
    acc = acc + jnp.dot(a2_ref[...], b2_ref[...], preferred_element_type=F32)
    o_ref[...] = acc


def matmul_concat2(a1, a2, b, *, tm=MM_TM, tn=MM_TN, name):
    m, k1 = a1.shape
    _, k2 = a2.shape
    assert k1 == k2
    _, n = b.shape
    return pl.pallas_call(
        _mm2_kernel,
        out_shape=jax.ShapeDtypeStruct((m, n), F32),
        grid=(m // tm, n // tn),
        in_specs=[pl.BlockSpec((tm, k1), lambda i, j: (i, 0)),
                  pl.BlockSpec((tm, k2), lambda i, j: (i, 0)),
                  pl.BlockSpec((k1, tn), lambda i, j: (0, j)),
                  pl.BlockSpec((k2, tn), lambda i, j: (1, j))],
        out_specs=pl.BlockSpec((tm, tn), lambda i, j: (i, j)),
        compiler_params=_params("parallel", "parallel"),
        name=name,
    )(a1, a2, b, b)


def _rms(x, gain):
    return x * lax.rsqrt(jnp.mean(x * x, axis=-1, keepdims=True) + NORM_EPS) * gain


def _prenorm_kernel(x_ref, g_ref, o_ref):
    o_ref[...] = _rms(x_ref[...], g_ref[...]).astype(o_ref.dtype)


def prenorm(x, gain):
    s, d = x.shape
    return pl.pallas_call(
        _prenorm_kernel,
        out_shape=jax.ShapeDtypeStruct((s, d), BF16),
        grid=(s // NORM_ROWS,),
        in_specs=[pl.BlockSpec((NORM_ROWS, d), lambda i: (i, 0)),
                  pl.BlockSpec((1, d), lambda i: (0, 0))],
        out_specs=pl.BlockSpec((NORM_ROWS, d), lambda i: (i, 0)),
        compiler_params=_params("parallel"),
        name="prenorm",
    )(x, gain.reshape(1, d))


def _resid_norm_kernel(h_ref, y_ref, gpost_ref, gpre_ref, h_out_ref, xn_ref):
    h = h_ref[...] + _rms(y_ref[...], gpost_ref[...])
    h_out_ref[...] = h
    xn_ref[...] = _rms(h, gpre_ref[...]).astype(xn_ref.dtype)


def _resid_kernel(h_ref, y_ref, gpost_ref, h_out_ref):
    h_out_ref[...] = h_ref[...] + _rms(y_ref[...], gpost_ref[...])


def resid_norm(h, y, gain_post, gain_pre):
    s, d = h.shape
    row = pl.BlockSpec((NORM_ROWS, d), lambda i: (i, 0))
    vec = pl.BlockSpec((1, d), lambda i: (0, 0))
    if gain_pre is None:
        return pl.pallas_call(
            _resid_kernel,
            out_shape=jax.ShapeDtypeStruct((s, d), F32),
            grid=(s // NORM_ROWS,),
            in_specs=[row, row, vec],
            out_specs=row,
            compiler_params=_params("parallel"),
            name="resid",
        )(h, y, gain_post.reshape(1, d)), None
    return pl.pallas_call(
        _resid_norm_kernel,
        out_shape=(jax.ShapeDtypeStruct((s, d), F32), jax.ShapeDtypeStruct((s, d), BF16)),
        grid=(s // NORM_ROWS,),
        in_specs=[row, row, vec, vec],
        out_specs=(row, row),
        compiler_params=_params("parallel"),
        name="resid_norm",
    )(h, y, gain_post.reshape(1, d), gain_pre.reshape(1, d))


def _silu(x):
    return x / (1.0 + jnp.exp(-x))


def _softplus(x):
    return jnp.maximum(x, 0.0) + jnp.log1p(jnp.exp(-jnp.abs(x)))


def _expand_heads(cols, lane_head):
    out = jnp.broadcast_to(cols[:, 0:1], lane_head.shape)
    for r in range(1, SSD_HEADS_PER_GROUP):
        out = jnp.where(lane_head == r, jnp.broadcast_to(cols[:, r:r + 1], lane_head.shape), out)
    return out


def _ssd_kernel(z_ref, xs_ref, b_ref, c_ref, dt_ref, wx_ref, wb_ref, wc_ref, bx_ref, bb_ref, bc_ref,
                dtb_ref, alog_ref, dskip_ref, gain_ref, o_ref, state_ref, cx_ref, cb_ref, cc_ref):
    L = SSD_CHUNK
    rows_total = xs_ref.shape[0]

    @pl.when(pl.program_id(1) == 0)
    def _():
        state_ref[...] = jnp.zeros_like(state_ref)
        cx_ref[...] = jnp.zeros_like(cx_ref)
        cb_ref[...] = jnp.zeros_like(cb_ref)
        cc_ref[...] = jnp.zeros_like(cc_ref)

    def conv_silu(x_ref, carry_ref, w_ref, bias_ref):
        x = x_ref[...].astype(F32)
        xe = jnp.concatenate([carry_ref[...], x], axis=0)
        w = w_ref[...]
        acc = x * w[SSD_CONV - 1:SSD_CONV, :] + bias_ref[...]
        for j in range(1, SSD_CONV):
            shifted = pltpu.roll(xe, j, axis=0)[V7X_SUBLANES:, :]
            acc = acc + shifted * w[SSD_CONV - 1 - j:SSD_CONV - j, :]
        carry_ref[...] = x[rows_total - V7X_SUBLANES:, :]
        return _silu(acc)

    xs = conv_silu(xs_ref, cx_ref, wx_ref, bx_ref)
    bm = conv_silu(b_ref, cb_ref, wb_ref, bb_ref)
    cm = conv_silu(c_ref, cc_ref, wc_ref, bc_ref)

    dt = _softplus(dt_ref[...] + dtb_ref[...])
    a = -jnp.exp(alog_ref[...]) * dt

    ri = lax.broadcasted_iota(jnp.int32, (L, L), 0)
    ci = lax.broadcasted_iota(jnp.int32, (L, L), 1)
    causal = ri >= ci
    tril = causal.astype(F32)
    lane_head = lax.broadcasted_iota(jnp.int32, (L, SSD_GROUP_WIDTH), 1) // SSD_HEAD_DIM

    for c in range(rows_total // L):
        sl = slice(c * L, (c + 1) * L)
        xs_c, bm_c, cm_c = xs[sl], bm[sl], cm[sl]
        a_cs = jnp.dot(tril, a[sl], precision=lax.Precision.HIGHEST, preferred_element_type=F32)
        dt_x = _expand_heads(dt[sl], lane_head)
        acs_x = _expand_heads(a_cs, lane_head)
        xdt = xs_c * dt_x
        cb = lax.dot_general(cm_c.astype(BF16), bm_c.astype(BF16), (((1,), (1,)), ((), ())),
                             preferred_element_type=F32)
        y = jnp.zeros((L, SSD_GROUP_WIDTH), F32)
        for r in range(SSD_HEADS_PER_GROUP):
            colmat = jnp.broadcast_to(a_cs[:, r:r + 1], (L, L))
            seg = colmat - colmat.T
            decay = jnp.exp(jnp.where(causal, seg, -jnp.inf))
            scores = (cb * decay).astype(BF16)
            x_r = jnp.where(lane_head == r, xdt, 0.0).astype(BF16)
            y = y + jnp.dot(scores, x_r, preferred_element_type=F32)
        a_last = acs_x[L - 1:L, :]
        xds = (xdt * jnp.exp(a_last - acs_x)).astype(BF16)
        st_new = jnp.dot(bm_c.T.astype(BF16), xds, preferred_element_type=F32)
        state = state_ref[...]
        y_off = jnp.dot(cm_c.astype(BF16), state.astype(BF16), preferred_element_type=F32) * jnp.exp(acs_x)
        state_ref[...] = state * jnp.exp(a_last) + st_new
        y = y + y_off + dskip_ref[...] * xs_c
        y = y * _silu(z_ref[sl, :].astype(F32))
        o_ref[sl, :] = _rms(y, gain_ref[...]).astype(o_ref.dtype)


def ssd_mixer(hp, side, conv_w, conv_b, dtb_pad, alog_pad, dskip_x, gain):
    s = hp.shape[0]
    gw, n = SSD_GROUP_WIDTH, SSD_STATE
    rows = SSD_ROWS

    def hp_spec(width, col0):
        return pl.BlockSpec((rows, width), lambda g, i: (i, col0 // width + g))

    def w_spec(nrows, width, col0):
        return pl.BlockSpec((nrows, width), lambda g, i: (0, col0 // width + g))

    xbc_b = SSD_WIDTH
    xbc_c = SSD_WIDTH + SSD_GROUPS * n
    return pl.pallas_call(
        _ssd_kernel,
        out_shape=jax.ShapeDtypeStruct((s, SSD_WIDTH), BF16),
        grid=(SSD_GROUPS, s // rows),
        in_specs=[hp_spec(gw, COL_Z), hp_spec(gw, COL_XS), hp_spec(n, COL_B), hp_spec(n, COL_C),
                  pl.BlockSpec((rows, V7X_LANES), lambda g, i: (i, g)),
                  w_spec(SSD_CONV, gw, 0), w_spec(SSD_CONV, n, xbc_b), w_spec(SSD_CONV, n, xbc_c),
                  w_spec(1, gw, 0), w_spec(1, n, xbc_b), w_spec(1, n, xbc_c),
                  pl.BlockSpec((1, V7X_LANES), lambda g, i: (0, g)),
                  pl.BlockSpec((1, V7X_LANES), lambda g, i: (0, g)),
                  w_spec(1, gw, 0), w_spec(1, gw, 0)],
        out_specs=pl.BlockSpec((rows, gw), lambda g, i: (i, g)),
        scratch_shapes=[pltpu.VMEM((n, gw), F32),
                        pltpu.VMEM((V7X_SUBLANES, gw), F32),
                        pltpu.VMEM((V7X_SUBLANES, n), F32),
                        pltpu.VMEM((V7X_SUBLANES, n), F32)],
        compiler_params=_params("parallel", "arbitrary"),
        name="ssd_mixer",
    )(hp, hp, hp, hp, side, conv_w, conv_w, conv_w, conv_b, conv_b, conv_b,
      dtb_pad, alog_pad, dskip_x, gain)


def _fox_prep_kernel(f_ref, bias_ref, c_ref, carry_ref):
    L = SSD_CHUNK

    @pl.when(pl.program_id(0) == 0)
    def _():
        carry_ref[...] = jnp.zeros_like(carry_ref)

    ri = lax.broadcasted_iota(jnp.int32, (L, L), 0)
    ci = lax.broadcasted_iota(jnp.int32, (L, L), 1)
    tril = (ri >= ci).astype(F32)
    carry = carry_ref[...]
    for c in range(f_ref.shape[0] // L):
        sl = slice(c * L, (c + 1) * L)
        log_f = -_softplus(-(f_ref[sl, :] + bias_ref[...]))
        cs = jnp.dot(tril, log_f, precision=lax.Precision.HIGHEST, preferred_element_type=F32) + carry
        c_ref[sl, :] = cs
        carry = cs[L - 1:L, :]
    carry_ref[...] = carry


def fox_prep(side, f_bias_pad):
    s = side.shape[0]
    rows = FOX_PREP_ROWS
    return pl.pallas_call(
        _fox_prep_kernel,
        out_shape=jax.ShapeDtypeStruct((s, V7X_LANES), F32),
        grid=(s // rows,),
        in_specs=[pl.BlockSpec((rows, V7X_LANES), lambda i: (i, SSD_GROUPS)),
                  pl.BlockSpec((1, V7X_LANES), lambda i: (0, 0))],
        out_specs=pl.BlockSpec((rows, V7X_LANES), lambda i: (i, 0)),
        scratch_shapes=[pltpu.VMEM((1, V7X_LANES), F32)],
        compiler_params=_params("arbitrary"),
        name="fox_prep",
    )(side, f_bias_pad)


def _fox_kernel(qi_ref, kj_ref, q_ref, k_ref, v_ref, cq_ref, ck_ref, o_ref,
                qs_ref, cqcol_ref, m_ref, l_ref, acc_ref):
    T = FOX_T
    hd = pl.program_id(0)
    p = pl.program_id(1)
    qi = qi_ref[p]
    kj = kj_ref[p]

    @pl.when(kj == 0)
    def _():
        scale = 1.0 / math.sqrt(ATT_HEAD_DIM)
        qs_ref[...] = (q_ref[...].astype(F32) * scale).astype(BF16)
        lane = lax.broadcasted_iota(jnp.int32, cq_ref.shape, 1)
        cqcol_ref[...] = jnp.sum(jnp.where(lane == hd, cq_ref[...], 0.0), axis=1, keepdims=True)
        m_ref[...] = jnp.full_like(m_ref, -jnp.inf)
        l_ref[...] = jnp.zeros_like(l_ref)
        acc_ref[...] = jnp.zeros_like(acc_ref)

    s = lax.dot_general(qs_ref[...], k_ref[...], (((1,), (1,)), ((), ())), preferred_element_type=F32)
    s = s + cqcol_ref[...] - ck_ref[...]
    row = qi * T + lax.broadcasted_iota(jnp.int32, (T, T), 0)
    col = kj * T + lax.broadcasted_iota(jnp.int32, (T, T), 1)
    s = jnp.where(col <= row, s, -jnp.inf)
    m_prev = m_ref[...]
    m_new = jnp.maximum(m_prev, jnp.max(s, axis=1, keepdims=True))
    alpha = jnp.exp(m_prev - m_new)
    pr = jnp.exp(s - m_new)
    l_ref[...] = alpha * l_ref[...] + jnp.sum(pr, axis=1, keepdims=True)
    acc_ref[...] = alpha * acc_ref[...] + jnp.dot(pr.astype(BF16), v_ref[...], preferred_element_type=F32)
    m_ref[...] = m_new

    @pl.when(kj == qi)
    def _():
        o_ref[...] = (acc_ref[...] / l_ref[...]).astype(o_ref.dtype)


def forgetting_attention(hp, c_cum, c_cum_t):
    s = hp.shape[0]
    T = FOX_T
    nq = s // T
    pairs = [(i, j) for i in range(nq) for j in range(i + 1)]
    qi = jnp.asarray(np.array([p[0] for p in pairs], np.int32))
    kj = jnp.asarray(np.array([p[1] for p in pairs], np.int32))
    d = ATT_HEAD_DIM
    grid_spec = pltpu.PrefetchScalarGridSpec(
        num_scalar_prefetch=2,
        grid=(ATT_HEADS, len(pairs)),
        in_specs=[pl.BlockSpec((T, d), lambda h, p, qi, kj: (qi[p], COL_Q // d + h)),
                  pl.BlockSpec((T, d), lambda h, p, qi, kj: (kj[p], COL_K // d + h)),
                  pl.BlockSpec((T, d), lambda h, p, qi, kj: (kj[p], COL_V // d + h)),
                  pl.BlockSpec((T, V7X_LANES), lambda h, p, qi, kj: (qi[p], 0)),
                  pl.BlockSpec((None, 1, T), lambda h, p, qi, kj: (h, 0, kj[p]))],
        out_specs=pl.BlockSpec((T, d), lambda h, p, qi, kj: (qi[p], h)),
        scratch_shapes=[pltpu.VMEM((T, d), BF16),
                        pltpu.VMEM((T, 1), F32),
                        pltpu.VMEM((T, 1), F32),
                        pltpu.VMEM((T, 1), F32),
                        pltpu.VMEM((T, d), F32)],
    )
    return pl.pallas_call(
        _fox_kernel,
        out_shape=jax.ShapeDtypeStruct((s, ATT_WIDTH), BF16),
        grid_spec=grid_spec,
        compiler_params=_params("parallel", "arbitrary"),
        name="fox_attention",
    )(qi, kj, hp, hp, hp, c_cum, c_cum_t)


def _sg_kernel(u_ref, v_ref, gain_ref, bias_ref, w_ref, bt_ref, o_ref):
    L = SG_CHUNK
    gw = SG_GROUP_WIDTH
    v = v_ref[...].astype(F32)
    mu = jnp.mean(v, axis=-1, keepdims=True)
    xc = v - mu
    var = jnp.mean(xc * xc, axis=-1, keepdims=True)
    vn = (xc * lax.rsqrt(var + NORM_EPS) * gain_ref[...] + bias_ref[...]).astype(BF16)
    ri = lax.broadcasted_iota(jnp.int32, (L, L), 0)
    ci = lax.broadcasted_iota(jnp.int32, (L, L), 1)
    causal = ri >= ci
    for g in range(SG_GROUPS):
        wg = jnp.where(causal, w_ref[g], 0.0).astype(BF16)
        bcol = bt_ref[:, g:g + 1]
        cols = slice(g * gw, (g + 1) * gw)
        for c in range(v_ref.shape[0] // L):
            rows = slice(c * L, (c + 1) * L)
            vm = jnp.dot(wg, vn[rows, cols], preferred_element_type=F32) + bcol
            o_ref[rows, cols] = (u_ref[rows, cols].astype(F32) * vm).astype(o_ref.dtype)


def spatial_gate(hg, ln_gain, ln_bias, w_s, b_s):
    s = hg.shape[0]
    rows = SG_ROWS
    w = SG_WIDTH
    return pl.pallas_call(
        _sg_kernel,
        out_shape=jax.ShapeDtypeStruct((s, w), BF16),
        grid=(s // rows,),
        in_specs=[pl.BlockSpec((rows, w), lambda i: (i, 0)),
                  pl.BlockSpec((rows, w), lambda i: (i, 1)),
                  pl.BlockSpec((1, w), lambda i: (0, 0)),
                  pl.BlockSpec((1, w), lambda i: (0, 0)),
                  pl.BlockSpec((SG_GROUPS, SG_CHUNK, SG_CHUNK), lambda i: (0, 0, 0)),
                  pl.BlockSpec((SG_CHUNK, SG_GROUPS), lambda i: (0, 0))],
        out_specs=pl.BlockSpec((rows, w), lambda i: (i, 0)),
        compiler_params=_params("parallel"),
        name="spatial_gate",
    )(hg, hg, ln_gain.reshape(1, w), ln_bias.reshape(1, w), w_s, b_s.T)


def _even_weights(w_in):
    dt0 = SSD_WIDTH + XBC_WIDTH
    f0 = dt0 + SSD_HEADS + 3 * ATT_WIDTH
    w_main = jnp.concatenate([w_in[:, :dt0], w_in[:, dt0 + SSD_HEADS:f0]], axis=1).astype(BF16)
    k = w_in.shape[0]
    w_dt = w_in[:, dt0:dt0 + SSD_HEADS].reshape(k, SSD_GROUPS, SSD_HEADS_PER_GROUP)
    w_dt = jnp.pad(w_dt, ((0, 0), (0, 0), (0, V7X_LANES - SSD_HEADS_PER_GROUP))).reshape(k, SSD_GROUPS * V7X_LANES)
    w_f = jnp.pad(w_in[:, f0:], ((0, 0), (0, V7X_LANES - ATT_HEADS)))
    w_side = jnp.concatenate([w_dt, w_f], axis=1).astype(BF16)
    return w_main, w_side


def _pad_heads_per_group(v):
    v = v.reshape(SSD_GROUPS, SSD_HEADS_PER_GROUP)
    v = jnp.pad(v, ((0, 0), (0, V7X_LANES - SSD_HEADS_PER_GROUP)))
    return v.reshape(1, SSD_GROUPS * V7X_LANES)


def _even_mixer(xn, w_in, w_out, conv_w, conv_b, dt_bias, a_log, d_skip, norm_gain, f_bias):
    w_main, w_side = _even_weights(w_in)
    hp = matmul(xn, w_main, name="even_in")
    side = matmul(xn, w_side, out_dtype=F32, tn=SIDE_WIDTH, name="even_side")
    y_ssd = ssd_mixer(hp, side, conv_w, conv_b.reshape(1, -1),
                      _pad_heads_per_group(dt_bias), _pad_heads_per_group(a_log),
                      jnp.repeat(d_skip, SSD_HEAD_DIM).reshape(1, -1), norm_gain.reshape(1, -1))
    f_bias_pad = jnp.pad(f_bias, (0, V7X_LANES - ATT_HEADS)).reshape(1, V7X_LANES)
    c_cum = fox_prep(side, f_bias_pad)
    c_cum_t = c_cum[:, :ATT_HEADS].T.reshape(ATT_HEADS, 1, -1)
    y_att = forgetting_attention(hp, c_cum, c_cum_t)
    return matmul_concat2(y_ssd, y_att, w_out.astype(BF16), name="even_out")


def _odd_mixer(xn, w_in, w_out, ln_gain, ln_bias, w_s, b_s):
    hg = matmul(xn, w_in.astype(BF16), act="gelu_tanh", name="odd_in")
    gated = spatial_gate(hg, ln_gain, ln_bias, w_s, b_s)
    return matmul(gated, w_out.astype(BF16), out_dtype=F32, name="odd_out")


def kernel(x, pre_mix_gain, post_mix_gain, pre_ffn_gain, post_ffn_gain, even_w_in, even_w_out, ssd_conv_w, ssd_conv_b, ssd_dt_bias, ssd_a_log, ssd_d, ssd_norm_gain, fox_f_bias, odd_w_in, odd_w_out, sg_ln_gain, sg_ln_bias, sg_w_s, sg_b_s, ffn_w_up, ffn_w_down):
    bsz, s, d = x.shape
    assert bsz == 1 and s == SEQ and d == D_MODEL
    h = x.reshape(s, d)
    xn = prenorm(h, pre_mix_gain[0])
    for layer in range(DEPTH):
        i = layer // 2
        if layer % 2 == 0:
            y = _even_mixer(xn, even_w_in[i], even_w_out[i], ssd_conv_w[i], ssd_conv_b[i], ssd_dt_bias[i],
                            ssd_a_log[i], ssd_d[i], ssd_norm_gain[i], fox_f_bias[i])
        else:
            y = _odd_mixer(xn, odd_w_in[i], odd_w_out[i], sg_ln_gain[i], sg_ln_bias[i], sg_w_s[i], sg_b_s[i])
        h, xn = resid_norm(h, y, post_mix_gain[layer], pre_ffn_gain[layer])
        hid = matmul(xn, ffn_w_up[layer].astype(BF16), act="relu2", name="ffn_up")
        y = matmul(hid, ffn_w_down[layer].astype(BF16), out_dtype=F32, name="ffn_down")
        next_gain = pre_mix_gain[layer + 1] if layer + 1 < DEPTH else None
        h, xn = resid_norm(h, y, post_ffn_gain[layer], next_gain)
    return h.reshape(bsz, s, d)
```

```python
import functools
import math

import jax
import jax.numpy as jnp
import numpy as np
from jax import lax
from jax.experimental import pallas as pl
from jax.experimental.pallas import tpu as pltpu

F32 = jnp.float32
BF16 = jnp.bfloat16

D_MODEL = 4096
SEQ = 16384
DEPTH = 4
SSD_WIDTH = 2048
SSD_HEAD_DIM = 64
SSD_HEADS = 32
SSD_GROUPS = 8
SSD_HEADS_PER_GROUP = SSD_HEADS // SSD_GROUPS
SSD_GROUP_WIDTH = SSD_WIDTH // SSD_GROUPS
SSD_STATE = 128
SSD_CONV = 4
SSD_CHUNK = 128
XBC_WIDTH = SSD_WIDTH + 2 * SSD_GROUPS * SSD_STATE
ATT_WIDTH = 2048
ATT_HEAD_DIM = 128
ATT_HEADS = 16
SG_WIDTH = 4096
SG_GROUPS = 16
SG_GROUP_WIDTH = SG_WIDTH // SG_GROUPS
SG_CHUNK = 128
FFN_HIDDEN = 4 * D_MODEL
NORM_EPS = 1e-6

V7X_LANES = 128
V7X_SUBLANES = 8
V7X_VMEM_BYTES = 64 * 1024 * 1024
VMEM_LIMIT = V7X_VMEM_BYTES - 8 * 1024 * 1024

COL_Z = 0
COL_XS = SSD_WIDTH
COL_B = COL_XS + SSD_WIDTH
COL_C = COL_B + SSD_GROUPS * SSD_STATE
COL_Q = COL_C + SSD_GROUPS * SSD_STATE
COL_K = COL_Q + ATT_WIDTH
COL_V = COL_K + ATT_WIDTH
EVEN_MAIN_WIDTH = COL_V + ATT_WIDTH
SIDE_WIDTH = V7X_LANES
SIDE_DT_LANE = 0
SIDE_F_LANE = SSD_HEADS
LOG2E = math.log2(math.e)

MM_TM = 1024
MM_TN = 1024
MM_TK = 4096
NORM_ROWS = 256
SSD_ROWS = 512
FOX_TQ = 1024
FOX_TK = 2048
FOX_ROW_CHUNK = 256
FOX_SCORES_AHEAD = 1
FOX_SCORES_AHEAD_DIAG = 2
FOX_PREP_ROWS = 2048
SG_ROWS = 256


def _params(*sem):
    return pltpu.CompilerParams(dimension_semantics=sem, vmem_limit_bytes=VMEM_LIMIT)


def _activation(y, act):
    if act == "relu2":
        r = jnp.maximum(y, 0.0)
        return r * r
    if act == "gelu_tanh":
        c = math.sqrt(2.0 / math.pi)
        return 0.5 * y * (1.0 + jnp.tanh(c * (y + 0.044715 * (y * y * y))))
    assert act is None
    return y


def _mm_kernel(a_ref, b_ref, o_ref, *scratch, act, nk):
    part = jnp.dot(a_ref[...], b_ref[...], preferred_element_type=F32)
    if nk == 1:
        o_ref[...] = _activation(part, act).astype(o_ref.dtype)
    else:
        acc_ref, = scratch
        k = pl.program_id(2)

        @pl.when(k == 0)
        def _():
            acc_ref[...] = part

        @pl.when(jnp.logical_and(k > 0, k < nk - 1))
        def _():
            acc_ref[...] += part

        @pl.when(k == nk - 1)
        def _():
            o_ref[...] = _activation(acc_ref[...] + part, act).astype(o_ref.dtype)


def matmul(a, b, *, act=None, out_dtype=BF16, tm=MM_TM, tn=MM_TN, tk=MM_TK, name):
    m, k = a.shape
    _, n = b.shape
    tn = min(tn, n)
    tk = min(tk, k)
    nk = k // tk
    assert m % tm == 0 and n % tn == 0 and k % tk == 0
    return pl.pallas_call(
        functools.partial(_mm_kernel, act=act, nk=nk),
        out_shape=jax.ShapeDtypeStruct((m, n), out_dtype),
        grid=(m // tm, n // tn, nk),
        in_specs=[pl.BlockSpec((tm, tk), lambda i, j, kk: (i, kk)),
                  pl.BlockSpec((tk, tn), lambda i, j, kk: (kk, j))],
        out_specs=pl.BlockSpec((tm, tn), lambda i, j, kk: (i, j)),
        scratch_shapes=[pltpu.VMEM((tm, tn), F32)] if nk > 1 else [],
        compiler_params=_params("parallel", "parallel", "arbitrary"),
        name=name,
    )(a, b)


def _mm2_kernel(a1_ref, a2_ref, b1_ref, b2_ref, o_ref):
    acc = jnp.dot(a1_ref[...], b1_ref[...], preferred_element_type=F32)
    acc = acc + jnp.dot(a2_ref[...], b2_ref[...], preferred_element_type=F32)
    o_ref[...] = acc.astype(o_ref.dtype)


def matmul_concat2(a1, a2, b, *, tm=MM_TM, tn=MM_TN, name):
    m, k1 = a1.shape
    _, k2 = a2.shape
    assert k1 == k2
    _, n = b.shape
    return pl.pallas_call(
        _mm2_kernel,
        out_shape=jax.ShapeDtypeStruct((m, n), BF16),
        grid=(m // tm, n // tn),
        in_specs=[pl.BlockSpec((tm, k1), lambda i, j: (i, 0)),
                  pl.BlockSpec((tm, k2), lambda i, j: (i, 0)),
                  pl.BlockSpec((k1, tn), lambda i, j: (0, j)),
                  pl.BlockSpec((k2, tn), lambda i, j: (1, j))],
        out_specs=pl.BlockSpec((tm, tn), lambda i, j: (i, j)),
        compiler_params=_params("parallel", "parallel"),
        name=name,
    )(a1, a2, b, b)


def _rms(x, gain):
    return x * lax.rsqrt(jnp.mean(x * x, axis=-1, keepdims=True) + NORM_EPS) * gain


def _prenorm_kernel(x_ref, g_ref, o_ref):
    o_ref[...] = _rms(x_ref[...], g_ref[...]).astype(o_ref.dtype)


def prenorm(x, gain):
    s, d = x.shape
    return pl.pallas_call(
        _prenorm_kernel,
        out_shape=jax.ShapeDtypeStruct((s, d), BF16),
        grid=(s // NORM_ROWS,),
        in_specs=[pl.BlockSpec((NORM_ROWS, d), lambda i: (i, 0)),
                  pl.BlockSpec((1, d), lambda i: (0, 0))],
        out_specs=pl.BlockSpec((NORM_ROWS, d), lambda i: (i, 0)),
        compiler_params=_params("parallel"),
        name="prenorm",
    )(x, gain.reshape(1, d))


def _resid_norm_kernel(h_ref, y_ref, gpost_ref, gpre_ref, h_out_ref, xn_ref):
    h = h_ref[...] + _rms(y_ref[...].astype(F32), gpost_ref[...])
    h_out_ref[...] = h
    xn_ref[...] = _rms(h, gpre_ref[...]).astype(xn_ref.dtype)


def _resid_kernel(h_ref, y_ref, gpost_ref, h_out_ref):
    h_out_ref[...] = h_ref[...] + _rms(y_ref[...].astype(F32), gpost_ref[...])


def resid_norm(h, y, gain_post, gain_pre):
    s, d = h.shape
    row = pl.BlockSpec((NORM_ROWS, d), lambda i: (i, 0))
    vec = pl.BlockSpec((1, d), lambda i: (0, 0))
    if gain_pre is None:
        return pl.pallas_call(
            _resid_kernel,
            out_shape=jax.ShapeDtypeStruct((s, d), F32),
            grid=(s // NORM_ROWS,),
            in_specs=[row, row, vec],
            out_specs=row,
            compiler_params=_params("parallel"),
            name="resid",
        )(h, y, gain_post.reshape(1, d)), None
    return pl.pallas_call(
        _resid_norm_kernel,
        out_shape=(jax.ShapeDtypeStruct((s, d), F32), jax.ShapeDtypeStruct((s, d), BF16)),
        grid=(s // NORM_ROWS,),
        in_specs=[row, row, vec, vec],
        out_specs=(row, row),
        compiler_params=_params("parallel"),
        name="resid_norm",
    )(h, y, gain_post.reshape(1, d), gain_pre.reshape(1, d))


def _silu(x):
    return x / (1.0 + jnp.exp(-x))


def _softplus(x):
    return jnp.maximum(x, 0.0) + jnp.log1p(jnp.exp(-jnp.abs(x)))


def _expand_heads(cols, lane_head):
    out = jnp.broadcast_to(cols[:, 0:1], lane_head.shape)
    for r in range(1, SSD_HEADS_PER_GROUP):
        out = jnp.where(lane_head == r, jnp.broadcast_to(cols[:, r:r + 1], lane_head.shape), out)
    return out


def _ssd_kernel(z_ref, xs_ref, b_ref, c_ref, dt_ref, wx_ref, wb_ref, wc_ref, bx_ref, bb_ref, bc_ref,
                dtb_ref, alog_ref, dskip_ref, gain_ref, o_ref, state_ref, cx_ref, cb_ref, cc_ref):
    L = SSD_CHUNK
    rows_total = xs_ref.shape[0]

    @pl.when(pl.program_id(1) == 0)
    def _():
        state_ref[...] = jnp.zeros_like(state_ref)
        cx_ref[...] = jnp.zeros_like(cx_ref)
        cb_ref[...] = jnp.zeros_like(cb_ref)
        cc_ref[...] = jnp.zeros_like(cc_ref)

    def conv_silu(x_ref, carry_ref, w_ref, bias_ref):
        x = x_ref[...].astype(F32)
        xe = jnp.concatenate([carry_ref[...], x], axis=0)
        w = w_ref[...]
        acc = x * w[SSD_CONV - 1:SSD_CONV, :] + bias_ref[...]
        for j in range(1, SSD_CONV):
            shifted = pltpu.roll(xe, j, axis=0)[V7X_SUBLANES:, :]
            acc = acc + shifted * w[SSD_CONV - 1 - j:SSD_CONV - j, :]
        carry_ref[...] = x[rows_total - V7X_SUBLANES:, :]
        return _silu(acc)

    xs = conv_silu(xs_ref, cx_ref, wx_ref, bx_ref)
    bm = conv_silu(b_ref, cb_ref, wb_ref, bb_ref)
    cm = conv_silu(c_ref, cc_ref, wc_ref, bc_ref)

    si = lax.broadcasted_iota(jnp.int32, (V7X_LANES, V7X_LANES), 0)
    sj = lax.broadcasted_iota(jnp.int32, (V7X_LANES, V7X_LANES), 1)
    first = SIDE_DT_LANE + pl.program_id(0) * SSD_HEADS_PER_GROUP
    sel = jnp.where(sj < SSD_HEADS_PER_GROUP, (si == first + sj).astype(F32), 0.0)
    dt_raw = jnp.dot(dt_ref[...], sel, precision=lax.Precision.HIGHEST, preferred_element_type=F32)
    dt = _softplus(dt_raw + dtb_ref[...])
    a = -jnp.exp(alog_ref[...]) * dt

    ri = lax.broadcasted_iota(jnp.int32, (L, L), 0)
    ci = lax.broadcasted_iota(jnp.int32, (L, L), 1)
    causal = ri >= ci
    tril = causal.astype(F32)
    lane_head = lax.broadcasted_iota(jnp.int32, (L, SSD_GROUP_WIDTH), 1) // SSD_HEAD_DIM

    for c in range(rows_total // L):
        sl = slice(c * L, (c + 1) * L)
        xs_c, bm_c, cm_c = xs[sl], bm[sl], cm[sl]
        a_cs = jnp.dot(tril, a[sl], precision=lax.Precision.HIGHEST, preferred_element_type=F32)
        dt_x = _expand_heads(dt[sl], lane_head)
        acs_x = _expand_heads(a_cs, lane_head)
        xdt = xs_c * dt_x
        cb = lax.dot_general(cm_c.astype(BF16), bm_c.astype(BF16), (((1,), (1,)), ((), ())),
                             preferred_element_type=F32)
        y = jnp.zeros((L, SSD_GROUP_WIDTH), F32)
        for r in range(SSD_HEADS_PER_GROUP):
            colmat = jnp.broadcast_to(a_cs[:, r:r + 1], (L, L))
            seg = colmat - colmat.T
            decay = jnp.exp(jnp.where(causal, seg, -jnp.inf))
            scores = (cb * decay).astype(BF16)
            x_r = jnp.where(lane_head == r, xdt, 0.0).astype(BF16)
            y = y + jnp.dot(scores, x_r, preferred_element_type=F32)
        a_last = acs_x[L - 1:L, :]
        xds = (xdt * jnp.exp(a_last - acs_x)).astype(BF16)
        st_new = jnp.dot(bm_c.T.astype(BF16), xds, preferred_element_type=F32)
        state = state_ref[...]
        y_off = jnp.dot(cm_c.astype(BF16), state.astype(BF16), preferred_element_type=F32) * jnp.exp(acs_x)
        state_ref[...] = state * jnp.exp(a_last) + st_new
        y = y + y_off + dskip_ref[...] * xs_c
        y = y * _silu(z_ref[sl, :].astype(F32))
        o_ref[sl, :] = _rms(y, gain_ref[...]).astype(o_ref.dtype)


def ssd_mixer(hp, side, conv_w, conv_b, dtb_pad, alog_pad, dskip_x, gain):
    s = hp.shape[0]
    gw, n = SSD_GROUP_WIDTH, SSD_STATE
    rows = SSD_ROWS

    def hp_spec(width, col0):
        return pl.BlockSpec((rows, width), lambda g, i: (i, col0 // width + g))

    def w_spec(nrows, width, col0):
        return pl.BlockSpec((nrows, width), lambda g, i: (0, col0 // width + g))

    xbc_b = SSD_WIDTH
    xbc_c = SSD_WIDTH + SSD_GROUPS * n
    return pl.pallas_call(
        _ssd_kernel,
        out_shape=jax.ShapeDtypeStruct((s, SSD_WIDTH), BF16),
        grid=(SSD_GROUPS, s // rows),
        in_specs=[hp_spec(gw, COL_Z), hp_spec(gw, COL_XS), hp_spec(n, COL_B), hp_spec(n, COL_C),
                  pl.BlockSpec((rows, SIDE_WIDTH), lambda g, i: (i, 0)),
                  w_spec(SSD_CONV, gw, 0), w_spec(SSD_CONV, n, xbc_b), w_spec(SSD_CONV, n, xbc_c),
                  w_spec(1, gw, 0), w_spec(1, n, xbc_b), w_spec(1, n, xbc_c),
                  pl.BlockSpec((1, V7X_LANES), lambda g, i: (0, g)),
                  pl.BlockSpec((1, V7X_LANES), lambda g, i: (0, g)),
                  w_spec(1, gw, 0), w_spec(1, gw, 0)],
        out_specs=pl.BlockSpec((rows, gw), lambda g, i: (i, g)),
        scratch_shapes=[pltpu.VMEM((n, gw), F32),
                        pltpu.VMEM((V7X_SUBLANES, gw), F32),
                        pltpu.VMEM((V7X_SUBLANES, n), F32),
                        pltpu.VMEM((V7X_SUBLANES, n), F32)],
        compiler_params=_params("parallel", "arbitrary"),
        name="ssd_mixer",
    )(hp, hp, hp, hp, side, conv_w, conv_w, conv_w, conv_b, conv_b, conv_b,
      dtb_pad, alog_pad, dskip_x, gain)


def _fox_prep_kernel(f_ref, bias_ref, c_ref, carry_ref):
    L = SSD_CHUNK

    @pl.when(pl.program_id(0) == 0)
    def _():
        carry_ref[...] = jnp.zeros_like(carry_ref)

    ri = lax.broadcasted_iota(jnp.int32, (L, L), 0)
    ci = lax.broadcasted_iota(jnp.int32, (L, L), 1)
    tril = (ri >= ci).astype(F32)
    carry = carry_ref[...]
    for c in range(f_ref.shape[0] // L):
        sl = slice(c * L, (c + 1) * L)
        log_f = -_softplus(-(f_ref[sl, :] + bias_ref[...]))
        cs = jnp.dot(tril, log_f, precision=lax.Precision.HIGHEST, preferred_element_type=F32) + carry
        c_ref[sl, :] = cs * LOG2E
        carry = cs[L - 1:L, :]
    carry_ref[...] = carry


def fox_prep(side, f_bias_pad):
    s = side.shape[0]
    rows = FOX_PREP_ROWS
    return pl.pallas_call(
        _fox_prep_kernel,
        out_shape=jax.ShapeDtypeStruct((s, V7X_LANES), F32),
        grid=(s // rows,),
        in_specs=[pl.BlockSpec((rows, SIDE_WIDTH), lambda i: (i, 0)),
                  pl.BlockSpec((1, V7X_LANES), lambda i: (0, 0))],
        out_specs=pl.BlockSpec((rows, V7X_LANES), lambda i: (i, 0)),
        scratch_shapes=[pltpu.VMEM((1, V7X_LANES), F32)],
        compiler_params=_params("arbitrary"),
        name="fox_prep",
    )(side, f_bias_pad)


def _fox_kernel(qi_ref, kj_ref, kind_ref, q_ref, k_ref, v_ref, cq_ref, ck_ref, o_ref,
                qs_ref, cqcol_ref, m_ref, acc_ref):
    hd = pl.program_id(0)
    p = pl.program_id(1)
    kj = kj_ref[p]
    kind = kind_ref[p]

    @pl.when(kj == 0)
    def _():
        scale = LOG2E / math.sqrt(ATT_HEAD_DIM)
        qs_ref[...] = (q_ref[...].astype(F32) * scale).astype(BF16)
        lane = lax.broadcasted_iota(jnp.int32, cq_ref.shape, 1)
        cqcol_ref[...] = jnp.sum(jnp.where(lane == SIDE_F_LANE + hd, cq_ref[...], 0.0), axis=1, keepdims=True)
        m_ref[...] = jnp.full_like(m_ref, -jnp.inf)
        acc_ref[...] = jnp.zeros_like(acc_ref)

    def step(diag_block):
        R = FOX_ROW_CHUNK
        n_chunks = FOX_TQ // R
        masked = diag_block is not None
        key0 = diag_block * FOX_TQ if masked else 0
        v_ones = jnp.concatenate([v_ref[...], jnp.ones(v_ref.shape, BF16)], axis=1)

        def n_keys(rc):
            return key0 + (rc + 1) * R if masked else FOX_TK

        def scores(rc):
            nk = n_keys(rc)
            t = lax.dot_general(qs_ref[rc * R:(rc + 1) * R, :], k_ref[0:nk, :], (((1,), (1,)), ((), ())),
                                preferred_element_type=F32) - ck_ref[:, 0:nk]
            if masked:
                row = key0 + rc * R + lax.broadcasted_iota(jnp.int32, (R, nk), 0)
                col = lax.broadcasted_iota(jnp.int32, (R, nk), 1)
                t = jnp.where(col <= row, t, -jnp.inf)
            return t

        ahead = FOX_SCORES_AHEAD_DIAG if masked else FOX_SCORES_AHEAD
        ts = [scores(c) for c in range(min(ahead, n_chunks))]
        for rc in range(n_chunks):
            rows = slice(rc * R, (rc + 1) * R)
            if rc + ahead < n_chunks:
                ts.append(scores(rc + ahead))
            t = ts[rc]
            cq = cqcol_ref[rows, :]
            m_prev = m_ref[rows, :]
            m_new = jnp.maximum(m_prev, jnp.max(t, axis=1, keepdims=True) + cq)
            alpha = jnp.exp2(m_prev - m_new)
            pr = jnp.exp2(t - (m_new - cq))
            acc_ref[rows, :] = alpha * acc_ref[rows, :] + jnp.dot(
                pr.astype(BF16), v_ones[0:n_keys(rc), :], preferred_element_type=F32)
            m_ref[rows, :] = m_new

    @pl.when(kind == 0)
    def _():
        step(None)

    for b in range(FOX_TK // FOX_TQ):
        @pl.when(kind == 1 + b)
        def _(b=b):
            step(b)
            d = ATT_HEAD_DIM
            o_ref[...] = (acc_ref[:, 0:d] / acc_ref[:, d:2 * d]).astype(o_ref.dtype)


def forgetting_attention(hp, c_cum, c_cum_t):
    s = hp.shape[0]
    tq, tk = FOX_TQ, FOX_TK
    kb = tk // tq
    pairs = [(i, j, 0 if j < i // kb else 1 + i % kb) for i in range(s // tq) for j in range(i // kb + 1)]
    qi, kj, kind = (jnp.asarray(np.array([p[c] for p in pairs], np.int32)) for c in range(3))
    d = ATT_HEAD_DIM
    grid_spec = pltpu.PrefetchScalarGridSpec(
        num_scalar_prefetch=3,
        grid=(ATT_HEADS, len(pairs)),
        in_specs=[pl.BlockSpec((tq, d), lambda h, p, qi, kj, kind: (qi[p], COL_Q // d + h)),
                  pl.BlockSpec((tk, d), lambda h, p, qi, kj, kind: (kj[p], COL_K // d + h)),
                  pl.BlockSpec((tk, d), lambda h, p, qi, kj, kind: (kj[p], COL_V // d + h)),
                  pl.BlockSpec((tq, V7X_LANES), lambda h, p, qi, kj, kind: (qi[p], 0)),
                  pl.BlockSpec((None, 1, tk), lambda h, p, qi, kj, kind: (h, 0, kj[p]))],
        out_specs=pl.BlockSpec((tq, d), lambda h, p, qi, kj, kind: (qi[p], h)),
        scratch_shapes=[pltpu.VMEM((tq, d), BF16),
                        pltpu.VMEM((tq, 1), F32),
                        pltpu.VMEM((tq, 1), F32),
                        pltpu.VMEM((tq, 2 * d), F32)],
    )
    return pl.pallas_call(
        _fox_kernel,
        out_shape=jax.ShapeDtypeStruct((s, ATT_WIDTH), BF16),
        grid_spec=grid_spec,
        compiler_params=_params("parallel", "arbitrary"),
        name="fox_attention",
    )(qi, kj, kind, hp, hp, hp, c_cum, c_cum_t)


def _sg_kernel(u_ref, v_ref, gain_ref, bias_ref, w_ref, bt_ref, o_ref):
    L = SG_CHUNK
    gw = SG_GROUP_WIDTH
    v = v_ref[...].astype(F32)
    mu = jnp.mean(v, axis=-1, keepdims=True)
    xc = v - mu
    var = jnp.mean(xc * xc, axis=-1, keepdims=True)
    vn = (xc * lax.rsqrt(var + NORM_EPS) * gain_ref[...] + bias_ref[...]).astype(BF16)
    ri = lax.broadcasted_iota(jnp.int32, (L, L), 0)
    ci = lax.broadcasted_iota(jnp.int32, (L, L), 1)
    causal = ri >= ci
    for g in range(SG_GROUPS):
        wg = jnp.where(causal, w_ref[g], 0.0).astype(BF16)
        bcol = bt_ref[:, g:g + 1]
        cols = slice(g * gw, (g + 1) * gw)
        for c in range(v_ref.shape[0] // L):
            rows = slice(c * L, (c + 1) * L)
            vm = jnp.dot(wg, vn[rows, cols], preferred_element_type=F32) + bcol
            o_ref[rows, cols] = (u_ref[rows, cols].astype(F32) * vm).astype(o_ref.dtype)


def spatial_gate(hg, ln_gain, ln_bias, w_s, b_s):
    s = hg.shape[0]
    rows = SG_ROWS
    w = SG_WIDTH
    return pl.pallas_call(
        _sg_kernel,
        out_shape=jax.ShapeDtypeStruct((s, w), BF16),
        grid=(s // rows,),
        in_specs=[pl.BlockSpec((rows, w), lambda i: (i, 0)),
                  pl.BlockSpec((rows, w), lambda i: (i, 1)),
                  pl.BlockSpec((1, w), lambda i: (0, 0)),
                  pl.BlockSpec((1, w), lambda i: (0, 0)),
                  pl.BlockSpec((SG_GROUPS, SG_CHUNK, SG_CHUNK), lambda i: (0, 0, 0)),
                  pl.BlockSpec((SG_CHUNK, SG_GROUPS), lambda i: (0, 0))],
        out_specs=pl.BlockSpec((rows, w), lambda i: (i, 0)),
        compiler_params=_params("parallel"),
        name="spatial_gate",
    )(hg, hg, ln_gain.reshape(1, w), ln_bias.reshape(1, w), w_s, b_s.T)


def _even_weights(w_in):
    dt0 = SSD_WIDTH + XBC_WIDTH
    f0 = dt0 + SSD_HEADS + 3 * ATT_WIDTH
    w_main = jnp.concatenate([w_in[:, :dt0], w_in[:, dt0 + SSD_HEADS:f0]], axis=1).astype(BF16)
    w_side = jnp.concatenate([w_in[:, dt0:dt0 + SSD_HEADS], w_in[:, f0:]], axis=1)
    w_side = jnp.pad(w_side, ((0, 0), (0, SIDE_WIDTH - SSD_HEADS - ATT_HEADS))).astype(BF16)
    return w_main, w_side


def _pad_heads_per_group(v):
    v = v.reshape(SSD_GROUPS, SSD_HEADS_PER_GROUP)
    v = jnp.pad(v, ((0, 0), (0, V7X_LANES - SSD_HEADS_PER_GROUP)))
    return v.reshape(1, SSD_GROUPS * V7X_LANES)


def _even_mixer(xn, w_in, w_out, conv_w, conv_b, dt_bias, a_log, d_skip, norm_gain, f_bias):
    w_main, w_side = _even_weights(w_in)
    hp = matmul(xn, w_main, name="even_in")
    side = matmul(xn, w_side, out_dtype=F32, tn=SIDE_WIDTH, name="even_side")
    y_ssd = ssd_mixer(hp, side, conv_w, conv_b.reshape(1, -1),
                      _pad_heads_per_group(dt_bias), _pad_heads_per_group(a_log),
                      jnp.repeat(d_skip, SSD_HEAD_DIM).reshape(1, -1), norm_gain.reshape(1, -1))
    f_bias_pad = jnp.pad(f_bias, (SIDE_F_LANE, V7X_LANES - SIDE_F_LANE - ATT_HEADS)).reshape(1, V7X_LANES)
    c_cum = fox_prep(side, f_bias_pad)
    c_cum_t = c_cum[:, SIDE_F_LANE:SIDE_F_LANE + ATT_HEADS].T.reshape(ATT_HEADS, 1, -1)
    y_att = forgetting_attention(hp, c_cum, c_cum_t)
    return matmul_concat2(y_ssd, y_att, w_out.astype(BF16), name="even_out")


def _odd_mixer(xn, w_in, w_out, ln_gain, ln_bias, w_s, b_s):
    hg = matmul(xn, w_in.astype(BF16), act="gelu_tanh", name="odd_in")
    gated = spatial_gate(hg, ln_gain, ln_bias, w_s, b_s)
    return matmul(gated, w_out.astype(BF16), name="odd_out")


def kernel(x, pre_mix_gain, post_mix_gain, pre_ffn_gain, post_ffn_gain, even_w_in, even_w_out, ssd_conv_w, ssd_conv_b, ssd_dt_bias, ssd_a_log, ssd_d, ssd_norm_gain, fox_f_bias, odd_w_in, odd_w_out, sg_ln_gain, sg_ln_bias, sg_w_s, sg_b_s, ffn_w_up, ffn_w_down):
    bsz, s, d = x.shape
    assert bsz == 1 and s == SEQ and d == D_MODEL
    h = x.reshape(s, d)
    xn = prenorm(h, pre_mix_gain[0])
    for layer in range(DEPTH):
        i = layer // 2
        if layer % 2 == 0:
            y = _even_mixer(xn, even_w_in[i], even_w_out[i], ssd_conv_w[i], ssd_conv_b[i], ssd_dt_bias[i],
                            ssd_a_log[i], ssd_d[i], ssd_norm_gain[i], fox_f_bias[i])
        else:
            y = _odd_mixer(xn, odd_w_in[i], odd_w_out[i], sg_ln_gain[i], sg_ln_bias[i], sg_w_s[i], sg_b_s[i])
        h, xn = resid_norm(h, y, post_mix_gain[layer], pre_ffn_gain[layer])
        hid = matmul(xn, ffn_w_up[layer].astype(BF16), act="relu2", name="ffn_up")
        y = matmul(hid, ffn_w_down[layer].astype(BF16), name="ffn_down")
        next_gain = pre_mix_gain[layer + 1] if layer + 1 < DEPTH else None
        h, xn = resid_norm(h, y, post_ffn_gain[layer], next_gain)
    return h.reshape(bsz, s, d)
```

```python
import functools
import math

import jax
import jax.numpy as jnp
import numpy as np
from jax import lax
from jax.experimental import pallas as pl
from jax.experimental.pallas import tpu as pltpu

F32 = jnp.float32
BF16 = jnp.bfloat16

D_MODEL = 4096
SEQ = 16384
DEPTH = 4
SSD_WIDTH = 2048
SSD_HEAD_DIM = 64
SSD_HEADS = 32
SSD_GROUPS = 8
SSD_HEADS_PER_GROUP = SSD_HEADS // SSD_GROUPS
SSD_GROUP_WIDTH = SSD_WIDTH // SSD_GROUPS
SSD_STATE = 128
SSD_CONV = 4
SSD_CHUNK = 128
XBC_WIDTH = SSD_WIDTH + 2 * SSD_GROUPS * SSD_STATE
ATT_WIDTH = 2048
ATT_HEAD_DIM = 128
ATT_HEADS = 16
SG_WIDTH = 4096
SG_GROUPS = 16
SG_GROUP_WIDTH = SG_WIDTH // SG_GROUPS
SG_CHUNK = 128
FFN_HIDDEN = 4 * D_MODEL
NORM_EPS = 1e-6

V7X_LANES = 128
V7X_SUBLANES = 8
V7X_VMEM_BYTES = 64 * 1024 * 1024
VMEM_LIMIT = V7X_VMEM_BYTES - 8 * 1024 * 1024

COL_Z = 0
COL_XS = SSD_WIDTH
COL_B = COL_XS + SSD_WIDTH
COL_C = COL_B + SSD_GROUPS * SSD_STATE
COL_Q = COL_C + SSD_GROUPS * SSD_STATE
COL_K = COL_Q + ATT_WIDTH
COL_V = COL_K + ATT_WIDTH
EVEN_MAIN_WIDTH = COL_V + ATT_WIDTH
SIDE_WIDTH = V7X_LANES
SIDE_DT_LANE = 0
SIDE_F_LANE = SSD_HEADS
LOG2E = math.log2(math.e)

MM_TM = 1024
MM_TN = 1024
MM_TK = 4096
MM_ROW_CHUNKS = 2
NORM_ROWS = 256
SSD_ROWS = 512
FOX_TQ = 2048
FOX_TK = 2048
FOX_ROW_CHUNK = 256
FOX_SCORES_AHEAD = 1
FOX_SCORES_AHEAD_DIAG = 2
FOX_PREP_ROWS = 2048
SG_ROWS = 256


def _params(*sem):
    return pltpu.CompilerParams(dimension_semantics=sem, vmem_limit_bytes=VMEM_LIMIT)


def _activation(y, act):
    if act == "relu2":
        r = jnp.maximum(y, 0.0)
        return r * r
    if act == "gelu_tanh":
        c = math.sqrt(2.0 / math.pi)
        return 0.5 * y * (1.0 + jnp.tanh(c * (y + 0.044715 * (y * y * y))))
    assert act is None
    return y


def _row_chunks(ref, n_chunks):
    rows = ref.shape[0] // n_chunks
    return [slice(c * rows, (c + 1) * rows) for c in range(n_chunks)]


def _mm_kernel(a_ref, b_ref, o_ref, *scratch, act, nk, n_chunks):
    chunks = _row_chunks(a_ref, n_chunks)

    def partial_product(rows):
        return jnp.dot(a_ref[rows, :], b_ref[...], preferred_element_type=F32)

    if nk == 1:
        for rows in chunks:
            o_ref[rows, :] = _activation(partial_product(rows), act).astype(o_ref.dtype)
    else:
        acc_ref, = scratch
        k = pl.program_id(2)

        @pl.when(k == 0)
        def _():
            for rows in chunks:
                acc_ref[rows, :] = partial_product(rows)

        @pl.when(jnp.logical_and(k > 0, k < nk - 1))
        def _():
            for rows in chunks:
                acc_ref[rows, :] += partial_product(rows)

        @pl.when(k == nk - 1)
        def _():
            for rows in chunks:
                o_ref[rows, :] = _activation(acc_ref[rows, :] + partial_product(rows), act).astype(o_ref.dtype)


def matmul(a, b, layer, *, act=None, out_dtype=BF16, tm=MM_TM, tn=MM_TN, tk=MM_TK, name):
    m, k = a.shape
    _, _, n = b.shape
    tn = min(tn, n)
    tk = min(tk, k)
    nk = k // tk
    assert m % tm == 0 and n % tn == 0 and k % tk == 0
    return pl.pallas_call(
        functools.partial(_mm_kernel, act=act, nk=nk, n_chunks=MM_ROW_CHUNKS if nk > 1 else 1),
        out_shape=jax.ShapeDtypeStruct((m, n), out_dtype),
        grid=(m // tm, n // tn, nk),
        in_specs=[pl.BlockSpec((tm, tk), lambda i, j, kk: (i, kk)),
                  pl.BlockSpec((None, tk, tn), lambda i, j, kk: (layer, kk, j))],
        out_specs=pl.BlockSpec((tm, tn), lambda i, j, kk: (i, j)),
        scratch_shapes=[pltpu.VMEM((tm, tn), F32)] if nk > 1 else [],
        compiler_params=_params("parallel", "parallel", "arbitrary"),
        name=name,
    )(a, b)


def _mm2_kernel(a1_ref, a2_ref, b1_ref, b2_ref, o_ref, *, n_chunks):
    for rows in _row_chunks(a1_ref, n_chunks):
        acc = jnp.dot(a1_ref[rows, :], b1_ref[...], preferred_element_type=F32)
        acc = acc + jnp.dot(a2_ref[rows, :], b2_ref[...], preferred_element_type=F32)
        o_ref[rows, :] = acc.astype(o_ref.dtype)


def matmul_concat2(a1, a2, b, layer, *, tm=MM_TM, tn=MM_TN, name):
    m, k1 = a1.shape
    _, k2 = a2.shape
    assert k1 == k2
    _, _, n = b.shape
    return pl.pallas_call(
        functools.partial(_mm2_kernel, n_chunks=1),
        out_shape=jax.ShapeDtypeStruct((m, n), BF16),
        grid=(m // tm, n // tn),
        in_specs=[pl.BlockSpec((tm, k1), lambda i, j: (i, 0)),
                  pl.BlockSpec((tm, k2), lambda i, j: (i, 0)),
                  pl.BlockSpec((None, k1, tn), lambda i, j: (layer, 0, j)),
                  pl.BlockSpec((None, k2, tn), lambda i, j: (layer, 1, j))],
        out_specs=pl.BlockSpec((tm, tn), lambda i, j: (i, j)),
        compiler_params=_params("parallel", "parallel"),
        name=name,
    )(a1, a2, b, b)


def _rms(x, gain):
    return x * lax.rsqrt(jnp.mean(x * x, axis=-1, keepdims=True) + NORM_EPS) * gain


def _prenorm_kernel(x_ref, g_ref, o_ref):
    o_ref[...] = _rms(x_ref[...], g_ref[...]).astype(o_ref.dtype)


def prenorm(x, gain):
    s, d = x.shape
    return pl.pallas_call(
        _prenorm_kernel,
        out_shape=jax.ShapeDtypeStruct((s, d), BF16),
        grid=(s // NORM_ROWS,),
        in_specs=[pl.BlockSpec((NORM_ROWS, d), lambda i: (i, 0)),
                  pl.BlockSpec((1, d), lambda i: (0, 0))],
        out_specs=pl.BlockSpec((NORM_ROWS, d), lambda i: (i, 0)),
        compiler_params=_params("parallel"),
        name="prenorm",
    )(x, gain.reshape(1, d))


def _resid_norm_kernel(h_ref, y_ref, gpost_ref, gpre_ref, h_out_ref, xn_ref):
    h = h_ref[...] + _rms(y_ref[...].astype(F32), gpost_ref[...])
    h_out_ref[...] = h
    xn_ref[...] = _rms(h, gpre_ref[...]).astype(xn_ref.dtype)


def _resid_kernel(h_ref, y_ref, gpost_ref, h_out_ref):
    h_out_ref[...] = h_ref[...] + _rms(y_ref[...].astype(F32), gpost_ref[...])


def resid_norm(h, y, gain_post, gain_pre):
    s, d = h.shape
    row = pl.BlockSpec((NORM_ROWS, d), lambda i: (i, 0))
    vec = pl.BlockSpec((1, d), lambda i: (0, 0))
    if gain_pre is None:
        return pl.pallas_call(
            _resid_kernel,
            out_shape=jax.ShapeDtypeStruct((s, d), F32),
            grid=(s // NORM_ROWS,),
            in_specs=[row, row, vec],
            out_specs=row,
            compiler_params=_params("parallel"),
            name="resid",
        )(h, y, gain_post.reshape(1, d)), None
    return pl.pallas_call(
        _resid_norm_kernel,
        out_shape=(jax.ShapeDtypeStruct((s, d), F32), jax.ShapeDtypeStruct((s, d), BF16)),
        grid=(s // NORM_ROWS,),
        in_specs=[row, row, vec, vec],
        out_specs=(row, row),
        compiler_params=_params("parallel"),
        name="resid_norm",
    )(h, y, gain_post.reshape(1, d), gain_pre.reshape(1, d))


def _silu(x):
    return x / (1.0 + jnp.exp(-x))


def _softplus(x):
    return jnp.maximum(x, 0.0) + jnp.log1p(jnp.exp(-jnp.abs(x)))


def _expand_heads(cols, lane_head):
    out = jnp.broadcast_to(cols[:, 0:1], lane_head.shape)
    for r in range(1, SSD_HEADS_PER_GROUP):
        out = jnp.where(lane_head == r, jnp.broadcast_to(cols[:, r:r + 1], lane_head.shape), out)
    return out


def _ssd_kernel(z_ref, xs_ref, b_ref, c_ref, dt_ref, wx_ref, wb_ref, wc_ref, bx_ref, bb_ref, bc_ref,
                dtb_ref, alog_ref, dskip_ref, gain_ref, o_ref, state_ref, cx_ref, cb_ref, cc_ref):
    L = SSD_CHUNK
    rows_total = xs_ref.shape[0]

    @pl.when(pl.program_id(1) == 0)
    def _():
        state_ref[...] = jnp.zeros_like(state_ref)
        cx_ref[...] = jnp.zeros_like(cx_ref)
        cb_ref[...] = jnp.zeros_like(cb_ref)
        cc_ref[...] = jnp.zeros_like(cc_ref)

    def conv_silu(x_ref, carry_ref, w_ref, bias_ref):
        x = x_ref[...].astype(F32)
        xe = jnp.concatenate([carry_ref[...], x], axis=0)
        w = w_ref[...]
        acc = x * w[SSD_CONV - 1:SSD_CONV, :] + bias_ref[...]
        for j in range(1, SSD_CONV):
            shifted = pltpu.roll(xe, j, axis=0)[V7X_SUBLANES:, :]
            acc = acc + shifted * w[SSD_CONV - 1 - j:SSD_CONV - j, :]
        carry_ref[...] = x[rows_total - V7X_SUBLANES:, :]
        return _silu(acc)

    xs = conv_silu(xs_ref, cx_ref, wx_ref, bx_ref)
    bm = conv_silu(b_ref, cb_ref, wb_ref, bb_ref)
    cm = conv_silu(c_ref, cc_ref, wc_ref, bc_ref)

    first = SIDE_DT_LANE + pl.program_id(0) * SSD_HEADS_PER_GROUP
    dt_raw = pltpu.roll(dt_ref[...], (V7X_LANES - first) % V7X_LANES, axis=1)
    dt = _softplus(dt_raw + dtb_ref[...])
    a = -jnp.exp(alog_ref[...]) * dt

    ri = lax.broadcasted_iota(jnp.int32, (L, L), 0)
    ci = lax.broadcasted_iota(jnp.int32, (L, L), 1)
    causal = ri >= ci
    tril = causal.astype(F32)
    lane_head = lax.broadcasted_iota(jnp.int32, (L, SSD_GROUP_WIDTH), 1) // SSD_HEAD_DIM

    for c in range(rows_total // L):
        sl = slice(c * L, (c + 1) * L)
        xs_c, bm_c, cm_c = xs[sl], bm[sl], cm[sl]
        a_cs = jnp.dot(tril, a[sl], precision=lax.Precision.HIGHEST, preferred_element_type=F32)
        dt_x = _expand_heads(dt[sl], lane_head)
        acs_x = _expand_heads(a_cs, lane_head)
        xdt = xs_c * dt_x
        cb = lax.dot_general(cm_c.astype(BF16), bm_c.astype(BF16), (((1,), (1,)), ((), ())),
                             preferred_element_type=F32)
        y = jnp.zeros((L, SSD_GROUP_WIDTH), F32)
        for r in range(SSD_HEADS_PER_GROUP):
            colmat = jnp.broadcast_to(a_cs[:, r:r + 1], (L, L))
            seg = colmat - colmat.T
            decay = jnp.exp(jnp.where(causal, seg, -jnp.inf))
            scores = (cb * decay).astype(BF16)
            x_r = jnp.where(lane_head == r, xdt, 0.0).astype(BF16)
            y = y + jnp.dot(scores, x_r, preferred_element_type=F32)
        a_last = acs_x[L - 1:L, :]
        xds = (xdt * jnp.exp(a_last - acs_x)).astype(BF16)
        st_new = jnp.dot(bm_c.T.astype(BF16), xds, preferred_element_type=F32)
        state = state_ref[...]
        y_off = jnp.dot(cm_c.astype(BF16), state.astype(BF16), preferred_element_type=F32) * jnp.exp(acs_x)
        state_ref[...] = state * jnp.exp(a_last) + st_new
        y = y + y_off + dskip_ref[...] * xs_c
        y = y * _silu(z_ref[sl, :].astype(F32))
        o_ref[sl, :] = _rms(y, gain_ref[...]).astype(o_ref.dtype)


def ssd_mixer(hp, side, conv_w, conv_b, dtb_pad, alog_pad, dskip_x, gain):
    s = hp.shape[0]
    gw, n = SSD_GROUP_WIDTH, SSD_STATE
    rows = SSD_ROWS

    def hp_spec(width, col0):
        return pl.BlockSpec((rows, width), lambda g, i: (i, col0 // width + g))

    def w_spec(nrows, width, col0):
        return pl.BlockSpec((nrows, width), lambda g, i: (0, col0 // width + g))

    xbc_b = SSD_WIDTH
    xbc_c = SSD_WIDTH + SSD_GROUPS * n
    return pl.pallas_call(
        _ssd_kernel,
        out_shape=jax.ShapeDtypeStruct((s, SSD_WIDTH), BF16),
        grid=(SSD_GROUPS, s // rows),
        in_specs=[hp_spec(gw, COL_Z), hp_spec(gw, COL_XS), hp_spec(n, COL_B), hp_spec(n, COL_C),
                  pl.BlockSpec((rows, SIDE_WIDTH), lambda g, i: (i, 0)),
                  w_spec(SSD_CONV, gw, 0), w_spec(SSD_CONV, n, xbc_b), w_spec(SSD_CONV, n, xbc_c),
                  w_spec(1, gw, 0), w_spec(1, n, xbc_b), w_spec(1, n, xbc_c),
                  pl.BlockSpec((1, V7X_LANES), lambda g, i: (0, g)),
                  pl.BlockSpec((1, V7X_LANES), lambda g, i: (0, g)),
                  w_spec(1, gw, 0), w_spec(1, gw, 0)],
        out_specs=pl.BlockSpec((rows, gw), lambda g, i: (i, g)),
        scratch_shapes=[pltpu.VMEM((n, gw), F32),
                        pltpu.VMEM((V7X_SUBLANES, gw), F32),
                        pltpu.VMEM((V7X_SUBLANES, n), F32),
                        pltpu.VMEM((V7X_SUBLANES, n), F32)],
        compiler_params=_params("parallel", "arbitrary"),
        name="ssd_mixer",
    )(hp, hp, hp, hp, side, conv_w, conv_w, conv_w, conv_b, conv_b, conv_b,
      dtb_pad, alog_pad, dskip_x, gain)


def _fox_prep_kernel(f_ref, bias_ref, c_ref, carry_ref):
    L = SSD_CHUNK

    @pl.when(pl.program_id(0) == 0)
    def _():
        carry_ref[...] = jnp.zeros_like(carry_ref)

    ri = lax.broadcasted_iota(jnp.int32, (L, L), 0)
    ci = lax.broadcasted_iota(jnp.int32, (L, L), 1)
    tril = (ri >= ci).astype(F32)
    carry = carry_ref[...]
    for c in range(f_ref.shape[0] // L):
        sl = slice(c * L, (c + 1) * L)
        log_f = -_softplus(-(f_ref[sl, :] + bias_ref[...]))
        cs = jnp.dot(tril, log_f, precision=lax.Precision.HIGHEST, preferred_element_type=F32) + carry
        c_ref[sl, :] = cs * LOG2E
        carry = cs[L - 1:L, :]
    carry_ref[...] = carry


def fox_prep(side, f_bias_pad):
    s = side.shape[0]
    rows = FOX_PREP_ROWS
    return pl.pallas_call(
        _fox_prep_kernel,
        out_shape=jax.ShapeDtypeStruct((s, V7X_LANES), F32),
        grid=(s // rows,),
        in_specs=[pl.BlockSpec((rows, SIDE_WIDTH), lambda i: (i, 0)),
                  pl.BlockSpec((1, V7X_LANES), lambda i: (0, 0))],
        out_specs=pl.BlockSpec((rows, V7X_LANES), lambda i: (i, 0)),
        scratch_shapes=[pltpu.VMEM((1, V7X_LANES), F32)],
        compiler_params=_params("arbitrary"),
        name="fox_prep",
    )(side, f_bias_pad)


def _fox_kernel(qi_ref, kj_ref, kind_ref, q_ref, k_ref, v_ref, cq_ref, ck_ref, o_ref,
                qs_ref, cqcol_ref, m_ref, acc_ref):
    hd = pl.program_id(0)
    p = pl.program_id(1)
    kj = kj_ref[p]
    kind = kind_ref[p]

    @pl.when(kj == 0)
    def _():
        scale = LOG2E / math.sqrt(ATT_HEAD_DIM)
        qs_ref[...] = (q_ref[...].astype(F32) * scale).astype(BF16)
        lane = lax.broadcasted_iota(jnp.int32, cq_ref.shape, 1)
        cqcol_ref[...] = jnp.sum(jnp.where(lane == SIDE_F_LANE + hd, cq_ref[...], 0.0), axis=1, keepdims=True)
        m_ref[...] = jnp.full_like(m_ref, -jnp.inf)
        acc_ref[...] = jnp.zeros_like(acc_ref)

    def step(diag_block):
        R = FOX_ROW_CHUNK
        n_chunks = FOX_TQ // R
        masked = diag_block is not None
        key0 = diag_block * FOX_TQ if masked else 0
        v_ones = jnp.concatenate([v_ref[...], jnp.ones(v_ref.shape, BF16)], axis=1)

        def n_keys(rc):
            return key0 + (rc + 1) * R if masked else FOX_TK

        def scores(rc):
            nk = n_keys(rc)
            t = lax.dot_general(qs_ref[rc * R:(rc + 1) * R, :], k_ref[0:nk, :], (((1,), (1,)), ((), ())),
                                preferred_element_type=F32) - ck_ref[:, 0:nk]
            if masked:
                row = key0 + rc * R + lax.broadcasted_iota(jnp.int32, (R, nk), 0)
                col = lax.broadcasted_iota(jnp.int32, (R, nk), 1)
                t = jnp.where(col <= row, t, -jnp.inf)
            return t

        ahead = FOX_SCORES_AHEAD_DIAG if masked else FOX_SCORES_AHEAD
        ts = [scores(c) for c in range(min(ahead, n_chunks))]
        for rc in range(n_chunks):
            rows = slice(rc * R, (rc + 1) * R)
            if rc + ahead < n_chunks:
                ts.append(scores(rc + ahead))
            t = ts[rc]
            cq = cqcol_ref[rows, :]
            m_prev = m_ref[rows, :]
            m_new = jnp.maximum(m_prev, jnp.max(t, axis=1, keepdims=True) + cq)
            alpha = jnp.exp2(m_prev - m_new)
            pr = jnp.exp2(t - (m_new - cq))
            acc_ref[rows, :] = alpha * acc_ref[rows, :] + jnp.dot(
                pr.astype(BF16), v_ones[0:n_keys(rc), :], preferred_element_type=F32)
            m_ref[rows, :] = m_new

    @pl.when(kind == 0)
    def _():
        step(None)

    for b in range(FOX_TK // FOX_TQ):
        @pl.when(kind == 1 + b)
        def _(b=b):
            step(b)
            d = ATT_HEAD_DIM
            o_ref[...] = (acc_ref[:, 0:d] / acc_ref[:, d:2 * d]).astype(o_ref.dtype)


def forgetting_attention(hp, c_cum, c_cum_t):
    s = hp.shape[0]
    tq, tk = FOX_TQ, FOX_TK
    kb = tk // tq
    pairs = [(i, j, 0 if j < i // kb else 1 + i % kb) for i in range(s // tq) for j in range(i // kb + 1)]
    qi, kj, kind = (jnp.asarray(np.array([p[c] for p in pairs], np.int32)) for c in range(3))
    d = ATT_HEAD_DIM
    grid_spec = pltpu.PrefetchScalarGridSpec(
        num_scalar_prefetch=3,
        grid=(ATT_HEADS, len(pairs)),
        in_specs=[pl.BlockSpec((tq, d), lambda h, p, qi, kj, kind: (qi[p], COL_Q // d + h)),
                  pl.BlockSpec((tk, d), lambda h, p, qi, kj, kind: (kj[p], COL_K // d + h)),
                  pl.BlockSpec((tk, d), lambda h, p, qi, kj, kind: (kj[p], COL_V // d + h)),
                  pl.BlockSpec((tq, V7X_LANES), lambda h, p, qi, kj, kind: (qi[p], 0)),
                  pl.BlockSpec((None, 1, tk), lambda h, p, qi, kj, kind: (h, 0, kj[p]))],
        out_specs=pl.BlockSpec((tq, d), lambda h, p, qi, kj, kind: (qi[p], h)),
        scratch_shapes=[pltpu.VMEM((tq, d), BF16),
                        pltpu.VMEM((tq, 1), F32),
                        pltpu.VMEM((tq, 1), F32),
                        pltpu.VMEM((tq, 2 * d), F32)],
    )
    return pl.pallas_call(
        _fox_kernel,
        out_shape=jax.ShapeDtypeStruct((s, ATT_WIDTH), BF16),
        grid_spec=grid_spec,
        compiler_params=_params("parallel", "arbitrary"),
        name="fox_attention",
    )(qi, kj, kind, hp, hp, hp, c_cum, c_cum_t)


def _sg_kernel(u_ref, v_ref, gain_ref, bias_ref, w_ref, bt_ref, o_ref):
    L = SG_CHUNK
    gw = SG_GROUP_WIDTH
    v = v_ref[...].astype(F32)
    mu = jnp.mean(v, axis=-1, keepdims=True)
    xc = v - mu
    var = jnp.mean(xc * xc, axis=-1, keepdims=True)
    vn = (xc * lax.rsqrt(var + NORM_EPS) * gain_ref[...] + bias_ref[...]).astype(BF16)
    ri = lax.broadcasted_iota(jnp.int32, (L, L), 0)
    ci = lax.broadcasted_iota(jnp.int32, (L, L), 1)
    causal = ri >= ci
    for g in range(SG_GROUPS):
        wg = jnp.where(causal, w_ref[g], 0.0).astype(BF16)
        bcol = bt_ref[:, g:g + 1]
        cols = slice(g * gw, (g + 1) * gw)
        for c in range(v_ref.shape[0] // L):
            rows = slice(c * L, (c + 1) * L)
            vm = jnp.dot(wg, vn[rows, cols], preferred_element_type=F32) + bcol
            o_ref[rows, cols] = (u_ref[rows, cols].astype(F32) * vm).astype(o_ref.dtype)


def spatial_gate(hg, ln_gain, ln_bias, w_s, b_s):
    s = hg.shape[0]
    rows = SG_ROWS
    w = SG_WIDTH
    return pl.pallas_call(
        _sg_kernel,
        out_shape=jax.ShapeDtypeStruct((s, w), BF16),
        grid=(s // rows,),
        in_specs=[pl.BlockSpec((rows, w), lambda i: (i, 0)),
                  pl.BlockSpec((rows, w), lambda i: (i, 1)),
                  pl.BlockSpec((1, w), lambda i: (0, 0)),
                  pl.BlockSpec((1, w), lambda i: (0, 0)),
                  pl.BlockSpec((SG_GROUPS, SG_CHUNK, SG_CHUNK), lambda i: (0, 0, 0)),
                  pl.BlockSpec((SG_CHUNK, SG_GROUPS), lambda i: (0, 0))],
        out_specs=pl.BlockSpec((rows, w), lambda i: (i, 0)),
        compiler_params=_params("parallel"),
        name="spatial_gate",
    )(hg, hg, ln_gain.reshape(1, w), ln_bias.reshape(1, w), w_s, b_s.T)


def _even_weights(w_in):
    dt0 = SSD_WIDTH + XBC_WIDTH
    f0 = dt0 + SSD_HEADS + 3 * ATT_WIDTH
    w_main = jnp.concatenate([w_in[..., :dt0], w_in[..., dt0 + SSD_HEADS:f0]], axis=-1).astype(BF16)
    w_side = jnp.concatenate([w_in[..., dt0:dt0 + SSD_HEADS], w_in[..., f0:]], axis=-1)
    w_side = jnp.pad(w_side, ((0, 0), (0, 0), (0, SIDE_WIDTH - SSD_HEADS - ATT_HEADS))).astype(BF16)
    return w_main, w_side


def _pad_heads_per_group(v):
    v = v.reshape(SSD_GROUPS, SSD_HEADS_PER_GROUP)
    v = jnp.pad(v, ((0, 0), (0, V7X_LANES - SSD_HEADS_PER_GROUP)))
    return v.reshape(1, SSD_GROUPS * V7X_LANES)


def _even_mixer(xn, i, w_main, w_side, w_out, conv_w, conv_b, dt_bias, a_log, d_skip, norm_gain, f_bias):
    hp = matmul(xn, w_main, i, name="even_in")
    side = matmul(xn, w_side, i, out_dtype=F32, name="even_side")
    y_ssd = ssd_mixer(hp, side, conv_w, conv_b.reshape(1, -1),
                      _pad_heads_per_group(dt_bias), _pad_heads_per_group(a_log),
                      jnp.repeat(d_skip, SSD_HEAD_DIM).reshape(1, -1), norm_gain.reshape(1, -1))
    f_bias_pad = jnp.pad(f_bias, (SIDE_F_LANE, V7X_LANES - SIDE_F_LANE - ATT_HEADS)).reshape(1, V7X_LANES)
    c_cum = fox_prep(side, f_bias_pad)
    c_cum_t = c_cum[:, SIDE_F_LANE:SIDE_F_LANE + ATT_HEADS].T.reshape(ATT_HEADS, 1, -1)
    y_att = forgetting_attention(hp, c_cum, c_cum_t)
    return matmul_concat2(y_ssd, y_att, w_out, i, name="even_out")


def _odd_mixer(xn, i, w_in, w_out, ln_gain, ln_bias, w_s, b_s):
    hg = matmul(xn, w_in, i, act="gelu_tanh", name="odd_in")
    gated = spatial_gate(hg, ln_gain, ln_bias, w_s, b_s)
    return matmul(gated, w_out, i, name="odd_out")


def kernel(x, pre_mix_gain, post_mix_gain, pre_ffn_gain, post_ffn_gain, even_w_in, even_w_out, ssd_conv_w, ssd_conv_b, ssd_dt_bias, ssd_a_log, ssd_d, ssd_norm_gain, fox_f_bias, odd_w_in, odd_w_out, sg_ln_gain, sg_ln_bias, sg_w_s, sg_b_s, ffn_w_up, ffn_w_down):
    bsz, s, d = x.shape
    assert bsz == 1 and s == SEQ and d == D_MODEL
    h = x.reshape(s, d)
    even_main, even_side = _even_weights(even_w_in)
    even_out, odd_in, odd_out = even_w_out.astype(BF16), odd_w_in.astype(BF16), odd_w_out.astype(BF16)
    ffn_up, ffn_down = ffn_w_up.astype(BF16), ffn_w_down.astype(BF16)
    xn = prenorm(h, pre_mix_gain[0])
    for layer in range(DEPTH):
        i = layer // 2
        if layer % 2 == 0:
            y = _even_mixer(xn, i, even_main, even_side, even_out, ssd_conv_w[i], ssd_conv_b[i], ssd_dt_bias[i],
                            ssd_a_log[i], ssd_d[i], ssd_norm_gain[i], fox_f_bias[i])
        else:
            y = _odd_mixer(xn, i, odd_in, odd_out, sg_ln_gain[i], sg_ln_bias[i], sg_w_s[i], sg_b_s[i])
        h, xn = resid_norm(h, y, post_mix_gain[layer], pre_ffn_gain[layer])
        hid = matmul(xn, ffn_up, layer, act="relu2", name="ffn_up")
        y = matmul(hid, ffn_down, layer, name="ffn_down")
        next_gain = pre_mix_gain[layer + 1] if layer + 1 < DEPTH else None
        h, xn = resid_norm(h, y, post_ffn_gain[layer], next_gain)
    return h.reshape(bsz, s, d)
```

```python
import functools
import math

import jax
import jax.numpy as jnp
import numpy as np
from jax import lax
from jax.experimental import pallas as pl
from jax.experimental.pallas import tpu as pltpu

F32 = jnp.float32
BF16 = jnp.bfloat16

D_MODEL = 4096
SEQ = 16384
DEPTH = 4
SSD_WIDTH = 2048
SSD_HEAD_DIM = 64
SSD_HEADS = 32
SSD_GROUPS = 8
SSD_HEADS_PER_GROUP = SSD_HEADS // SSD_GROUPS
SSD_GROUP_WIDTH = SSD_WIDTH // SSD_GROUPS
SSD_STATE = 128
SSD_CONV = 4
SSD_CHUNK = 128
XBC_WIDTH = SSD_WIDTH + 2 * SSD_GROUPS * SSD_STATE
ATT_WIDTH = 2048
ATT_HEAD_DIM = 128
ATT_HEADS = 16
SG_WIDTH = 4096
SG_GROUPS = 16
SG_GROUP_WIDTH = SG_WIDTH // SG_GROUPS
SG_CHUNK = 128
FFN_HIDDEN = 4 * D_MODEL
NORM_EPS = 1e-6

V7X_LANES = 128
V7X_SUBLANES = 8
V7X_BF16_SUBLANES = 16
V7X_VMEM_BYTES = 64 * 1024 * 1024
VMEM_LIMIT = V7X_VMEM_BYTES - 8 * 1024 * 1024

COL_Z = 0
COL_XS = SSD_WIDTH
COL_B = COL_XS + SSD_WIDTH
COL_C = COL_B + SSD_GROUPS * SSD_STATE
COL_Q = 0
COL_K = COL_Q + ATT_WIDTH
COL_V = COL_K + ATT_WIDTH
SIDE_WIDTH = V7X_LANES
SIDE_DT_LANE = 0
SIDE_F_LANE = SSD_HEADS
LOG2E = math.log2(math.e)

MM_TM = 1024
MM_TN = 1024
MM_TK = 4096
MM_ROW_CHUNKS = 2
NORM_ROWS = 256
SSD_ROWS = 512
FOX_TQ = 2048
FOX_TK = 2048
FOX_ROW_CHUNK = 256
FOX_SCORES_AHEAD = 1
FOX_SCORES_AHEAD_DIAG = 2
FOX_PREP_ROWS = 2048
SG_ROWS = 256


def _params(*sem):
    return pltpu.CompilerParams(dimension_semantics=sem, vmem_limit_bytes=VMEM_LIMIT)


def _activation(y, act):
    if act == "relu2":
        r = jnp.maximum(y, 0.0)
        return r * r
    if act == "gelu_tanh":
        c = math.sqrt(2.0 / math.pi)
        return 0.5 * y * (1.0 + jnp.tanh(c * (y + 0.044715 * (y * y * y))))
    assert act is None
    return y


def _row_chunks(ref, n_chunks):
    rows = ref.shape[0] // n_chunks
    return [slice(c * rows, (c + 1) * rows) for c in range(n_chunks)]


def _mm_kernel(*refs, act, nk, n_chunks, cast_blocks):
    if cast_blocks:
        a_ref, b_ref, src_ref, o_ref, dst_ref, *scratch = refs
        step = (pl.program_id(0) * pl.num_programs(1) + pl.program_id(1)) * nk + pl.program_id(2)

        @pl.when(step < cast_blocks)
        def _():
            dst_ref[...] = src_ref[...].astype(dst_ref.dtype)
    else:
        a_ref, b_ref, o_ref, *scratch = refs
    chunks = _row_chunks(a_ref, n_chunks)

    def partial_product(rows):
        return jnp.dot(a_ref[rows, :], b_ref[...], preferred_element_type=F32)

    if nk == 1:
        for rows in chunks:
            o_ref[rows, :] = _activation(partial_product(rows), act).astype(o_ref.dtype)
    else:
        acc_ref, = scratch
        k = pl.program_id(2)

        @pl.when(k == 0)
        def _():
            for rows in chunks:
                acc_ref[rows, :] = partial_product(rows)

        @pl.when(jnp.logical_and(k > 0, k < nk - 1))
        def _():
            for rows in chunks:
                acc_ref[rows, :] += partial_product(rows)

        @pl.when(k == nk - 1)
        def _():
            for rows in chunks:
                o_ref[rows, :] = _activation(acc_ref[rows, :] + partial_product(rows), act).astype(o_ref.dtype)


def _cast_block_count(steps, rows):
    nb = 1
    while nb * 2 <= steps and rows % (nb * 2) == 0 and (rows // (nb * 2)) % V7X_BF16_SUBLANES == 0:
        nb *= 2
    return nb


def matmul(a, b, layer, *, act=None, out_dtype=BF16, tm=MM_TM, tn=MM_TN, tk=MM_TK, name, cast=None):
    m, k = a.shape
    _, _, n = b.shape
    tn = min(tn, n)
    tk = min(tk, k)
    nk = k // tk
    assert m % tm == 0 and n % tn == 0 and k % tk == 0
    grid = (m // tm, n // tn, nk)
    in_specs = [pl.BlockSpec((tm, tk), lambda i, j, kk: (i, kk)),
                pl.BlockSpec((None, tk, tn), lambda i, j, kk: (layer, kk, j))]
    out_specs = pl.BlockSpec((tm, tn), lambda i, j, kk: (i, j))
    out_shape = jax.ShapeDtypeStruct((m, n), out_dtype)
    operands = [a, b]
    cast_blocks = 0
    if cast is not None:
        w, w_layer = cast
        _, rows, cols = w.shape
        cast_blocks = _cast_block_count(grid[0] * grid[1] * grid[2], rows)

        def block_of_step(i, j, kk):
            return jnp.minimum((i * grid[1] + j) * nk + kk, cast_blocks - 1)

        in_specs.append(pl.BlockSpec((None, rows // cast_blocks, cols),
                                     lambda i, j, kk: (w_layer, block_of_step(i, j, kk), 0)))
        out_specs = (out_specs, pl.BlockSpec((None, rows // cast_blocks, cols),
                                             lambda i, j, kk: (0, block_of_step(i, j, kk), 0)))
        out_shape = (out_shape, jax.ShapeDtypeStruct((1, rows, cols), BF16))
        operands.append(w)
    semantics = ("arbitrary",) * 3 if cast_blocks else ("parallel", "parallel", "arbitrary")
    return pl.pallas_call(
        functools.partial(_mm_kernel, act=act, nk=nk, n_chunks=MM_ROW_CHUNKS if nk > 1 else 1,
                          cast_blocks=cast_blocks),
        out_shape=out_shape,
        grid=grid,
        in_specs=in_specs,
        out_specs=out_specs,
        scratch_shapes=[pltpu.VMEM((tm, tn), F32)] if nk > 1 else [],
        compiler_params=_params(*semantics),
        name=name,
    )(*operands)


def _mm2_kernel(a1_ref, a2_ref, b1_ref, b2_ref, o_ref, *, n_chunks):
    for rows in _row_chunks(a1_ref, n_chunks):
        acc = jnp.dot(a1_ref[rows, :], b1_ref[...], preferred_element_type=F32)
        acc = acc + jnp.dot(a2_ref[rows, :], b2_ref[...], preferred_element_type=F32)
        o_ref[rows, :] = acc.astype(o_ref.dtype)


def matmul_concat2(a1, a2, b, layer, *, tm=MM_TM, tn=MM_TN, name):
    m, k1 = a1.shape
    _, k2 = a2.shape
    assert k1 == k2
    _, _, n = b.shape
    return pl.pallas_call(
        functools.partial(_mm2_kernel, n_chunks=1),
        out_shape=jax.ShapeDtypeStruct((m, n), BF16),
        grid=(m // tm, n // tn),
        in_specs=[pl.BlockSpec((tm, k1), lambda i, j: (i, 0)),
                  pl.BlockSpec((tm, k2), lambda i, j: (i, 0)),
                  pl.BlockSpec((None, k1, tn), lambda i, j: (layer, 0, j)),
                  pl.BlockSpec((None, k2, tn), lambda i, j: (layer, 1, j))],
        out_specs=pl.BlockSpec((tm, tn), lambda i, j: (i, j)),
        compiler_params=_params("parallel", "parallel"),
        name=name,
    )(a1, a2, b, b)


def _rms(x, gain):
    return x * lax.rsqrt(jnp.mean(x * x, axis=-1, keepdims=True) + NORM_EPS) * gain


def _prenorm_kernel(x_ref, g_ref, o_ref):
    o_ref[...] = _rms(x_ref[...], g_ref[...]).astype(o_ref.dtype)


def prenorm(x, gain):
    s, d = x.shape
    return pl.pallas_call(
        _prenorm_kernel,
        out_shape=jax.ShapeDtypeStruct((s, d), BF16),
        grid=(s // NORM_ROWS,),
        in_specs=[pl.BlockSpec((NORM_ROWS, d), lambda i: (i, 0)),
                  pl.BlockSpec((1, d), lambda i: (0, 0))],
        out_specs=pl.BlockSpec((NORM_ROWS, d), lambda i: (i, 0)),
        compiler_params=_params("parallel"),
        name="prenorm",
    )(x, gain.reshape(1, d))


def _resid_norm_kernel(h_ref, y_ref, gpost_ref, gpre_ref, h_out_ref, xn_ref):
    h = h_ref[...] + _rms(y_ref[...].astype(F32), gpost_ref[...])
    h_out_ref[...] = h
    xn_ref[...] = _rms(h, gpre_ref[...]).astype(xn_ref.dtype)


def _resid_kernel(h_ref, y_ref, gpost_ref, h_out_ref):
    h_out_ref[...] = h_ref[...] + _rms(y_ref[...].astype(F32), gpost_ref[...])


def resid_norm(h, y, gain_post, gain_pre):
    s, d = h.shape
    row = pl.BlockSpec((NORM_ROWS, d), lambda i: (i, 0))
    vec = pl.BlockSpec((1, d), lambda i: (0, 0))
    if gain_pre is None:
        return pl.pallas_call(
            _resid_kernel,
            out_shape=jax.ShapeDtypeStruct((s, d), F32),
            grid=(s // NORM_ROWS,),
            in_specs=[row, row, vec],
            out_specs=row,
            compiler_params=_params("parallel"),
            name="resid",
        )(h, y, gain_post.reshape(1, d)), None
    return pl.pallas_call(
        _resid_norm_kernel,
        out_shape=(jax.ShapeDtypeStruct((s, d), F32), jax.ShapeDtypeStruct((s, d), BF16)),
        grid=(s // NORM_ROWS,),
        in_specs=[row, row, vec, vec],
        out_specs=(row, row),
        compiler_params=_params("parallel"),
        name="resid_norm",
    )(h, y, gain_post.reshape(1, d), gain_pre.reshape(1, d))


def _silu(x):
    return x / (1.0 + jnp.exp(-x))


def _softplus(x):
    return jnp.maximum(x, 0.0) + jnp.log1p(jnp.exp(-jnp.abs(x)))


def _expand_heads(cols, lane_head):
    out = jnp.broadcast_to(cols[:, 0:1], lane_head.shape)
    for r in range(1, SSD_HEADS_PER_GROUP):
        out = jnp.where(lane_head == r, jnp.broadcast_to(cols[:, r:r + 1], lane_head.shape), out)
    return out


def _ssd_kernel(z_ref, xs_ref, b_ref, c_ref, dt_ref, wx_ref, wb_ref, wc_ref, bx_ref, bb_ref, bc_ref,
                dtb_ref, alog_ref, dskip_ref, gain_ref, o_ref, state_ref, cx_ref, cb_ref, cc_ref):
    L = SSD_CHUNK
    rows_total = xs_ref.shape[0]

    @pl.when(pl.program_id(1) == 0)
    def _():
        state_ref[...] = jnp.zeros_like(state_ref)
        cx_ref[...] = jnp.zeros_like(cx_ref)
        cb_ref[...] = jnp.zeros_like(cb_ref)
        cc_ref[...] = jnp.zeros_like(cc_ref)

    def conv_silu(x_ref, carry_ref, w_ref, bias_ref):
        x = x_ref[...].astype(F32)
        xe = jnp.concatenate([carry_ref[...], x], axis=0)
        w = w_ref[...]
        acc = x * w[SSD_CONV - 1:SSD_CONV, :] + bias_ref[...]
        for j in range(1, SSD_CONV):
            shifted = pltpu.roll(xe, j, axis=0)[V7X_SUBLANES:, :]
            acc = acc + shifted * w[SSD_CONV - 1 - j:SSD_CONV - j, :]
        carry_ref[...] = x[rows_total - V7X_SUBLANES:, :]
        return _silu(acc)

    xs = conv_silu(xs_ref, cx_ref, wx_ref, bx_ref)
    bm = conv_silu(b_ref, cb_ref, wb_ref, bb_ref)
    cm = conv_silu(c_ref, cc_ref, wc_ref, bc_ref)

    first = SIDE_DT_LANE + pl.program_id(0) * SSD_HEADS_PER_GROUP
    dt_raw = pltpu.roll(dt_ref[...], (V7X_LANES - first) % V7X_LANES, axis=1)
    dt = _softplus(dt_raw + dtb_ref[...])
    a = -jnp.exp(alog_ref[...]) * dt

    ri = lax.broadcasted_iota(jnp.int32, (L, L), 0)
    ci = lax.broadcasted_iota(jnp.int32, (L, L), 1)
    causal = ri >= ci
    tril = causal.astype(F32)
    lane_head = lax.broadcasted_iota(jnp.int32, (L, SSD_GROUP_WIDTH), 1) // SSD_HEAD_DIM

    for c in range(rows_total // L):
        sl = slice(c * L, (c + 1) * L)
        xs_c, bm_c, cm_c = xs[sl], bm[sl], cm[sl]
        a_cs = jnp.dot(tril, a[sl], precision=lax.Precision.HIGHEST, preferred_element_type=F32)
        dt_x = _expand_heads(dt[sl], lane_head)
        acs_x = _expand_heads(a_cs, lane_head)
        xdt = xs_c * dt_x
        cb = lax.dot_general(cm_c.astype(BF16), bm_c.astype(BF16), (((1,), (1,)), ((), ())),
                             preferred_element_type=F32)
        y = jnp.zeros((L, SSD_GROUP_WIDTH), F32)
        for r in range(SSD_HEADS_PER_GROUP):
            colmat = jnp.broadcast_to(a_cs[:, r:r + 1], (L, L))
            seg = colmat - colmat.T
            decay = jnp.exp(jnp.where(causal, seg, -jnp.inf))
            scores = (cb * decay).astype(BF16)
            x_r = jnp.where(lane_head == r, xdt, 0.0).astype(BF16)
            y = y + jnp.dot(scores, x_r, preferred_element_type=F32)
        a_last = acs_x[L - 1:L, :]
        xds = (xdt * jnp.exp(a_last - acs_x)).astype(BF16)
        st_new = jnp.dot(bm_c.T.astype(BF16), xds, preferred_element_type=F32)
        state = state_ref[...]
        y_off = jnp.dot(cm_c.astype(BF16), state.astype(BF16), preferred_element_type=F32) * jnp.exp(acs_x)
        state_ref[...] = state * jnp.exp(a_last) + st_new
        y = y + y_off + dskip_ref[...] * xs_c
        y = y * _silu(z_ref[sl, :].astype(F32))
        o_ref[sl, :] = _rms(y, gain_ref[...]).astype(o_ref.dtype)


def ssd_mixer(hp, side, conv_w, conv_b, dtb_pad, alog_pad, dskip_x, gain):
    s = hp.shape[0]
    gw, n = SSD_GROUP_WIDTH, SSD_STATE
    rows = SSD_ROWS

    def hp_spec(width, col0):
        return pl.BlockSpec((rows, width), lambda g, i: (i, col0 // width + g))

    def w_spec(nrows, width, col0):
        return pl.BlockSpec((nrows, width), lambda g, i: (0, col0 // width + g))

    xbc_b = SSD_WIDTH
    xbc_c = SSD_WIDTH + SSD_GROUPS * n
    return pl.pallas_call(
        _ssd_kernel,
        out_shape=jax.ShapeDtypeStruct((s, SSD_WIDTH), BF16),
        grid=(SSD_GROUPS, s // rows),
        in_specs=[hp_spec(gw, COL_Z), hp_spec(gw, COL_XS), hp_spec(n, COL_B), hp_spec(n, COL_C),
                  pl.BlockSpec((rows, SIDE_WIDTH), lambda g, i: (i, 0)),
                  w_spec(SSD_CONV, gw, 0), w_spec(SSD_CONV, n, xbc_b), w_spec(SSD_CONV, n, xbc_c),
                  w_spec(1, gw, 0), w_spec(1, n, xbc_b), w_spec(1, n, xbc_c),
                  pl.BlockSpec((1, V7X_LANES), lambda g, i: (0, g)),
                  pl.BlockSpec((1, V7X_LANES), lambda g, i: (0, g)),
                  w_spec(1, gw, 0), w_spec(1, gw, 0)],
        out_specs=pl.BlockSpec((rows, gw), lambda g, i: (i, g)),
        scratch_shapes=[pltpu.VMEM((n, gw), F32),
                        pltpu.VMEM((V7X_SUBLANES, gw), F32),
                        pltpu.VMEM((V7X_SUBLANES, n), F32),
                        pltpu.VMEM((V7X_SUBLANES, n), F32)],
        compiler_params=_params("parallel", "arbitrary"),
        name="ssd_mixer",
    )(hp, hp, hp, hp, side, conv_w, conv_w, conv_w, conv_b, conv_b, conv_b,
      dtb_pad, alog_pad, dskip_x, gain)


def _fox_prep_kernel(f_ref, bias_ref, c_ref, carry_ref):
    L = SSD_CHUNK

    @pl.when(pl.program_id(0) == 0)
    def _():
        carry_ref[...] = jnp.zeros_like(carry_ref)

    ri = lax.broadcasted_iota(jnp.int32, (L, L), 0)
    ci = lax.broadcasted_iota(jnp.int32, (L, L), 1)
    tril = (ri >= ci).astype(F32)
    carry = carry_ref[...]
    for c in range(f_ref.shape[0] // L):
        sl = slice(c * L, (c + 1) * L)
        log_f = -_softplus(-(f_ref[sl, :] + bias_ref[...]))
        cs = jnp.dot(tril, log_f, precision=lax.Precision.HIGHEST, preferred_element_type=F32) + carry
        c_ref[sl, :] = cs * LOG2E
        carry = cs[L - 1:L, :]
    carry_ref[...] = carry


def fox_prep(side, f_bias_pad):
    s = side.shape[0]
    rows = FOX_PREP_ROWS
    return pl.pallas_call(
        _fox_prep_kernel,
        out_shape=jax.ShapeDtypeStruct((s, V7X_LANES), F32),
        grid=(s // rows,),
        in_specs=[pl.BlockSpec((rows, SIDE_WIDTH), lambda i: (i, 0)),
                  pl.BlockSpec((1, V7X_LANES), lambda i: (0, 0))],
        out_specs=pl.BlockSpec((rows, V7X_LANES), lambda i: (i, 0)),
        scratch_shapes=[pltpu.VMEM((1, V7X_LANES), F32)],
        compiler_params=_params("arbitrary"),
        name="fox_prep",
    )(side, f_bias_pad)


def _fox_kernel(qi_ref, kj_ref, kind_ref, q_ref, k_ref, v_ref, cq_ref, ck_ref, o_ref,
                qs_ref, cqcol_ref, m_ref, acc_ref):
    hd = pl.program_id(0)
    p = pl.program_id(1)
    kj = kj_ref[p]
    kind = kind_ref[p]

    @pl.when(kj == 0)
    def _():
        scale = LOG2E / math.sqrt(ATT_HEAD_DIM)
        qs_ref[...] = (q_ref[...].astype(F32) * scale).astype(BF16)
        lane = lax.broadcasted_iota(jnp.int32, cq_ref.shape, 1)
        cqcol_ref[...] = jnp.sum(jnp.where(lane == SIDE_F_LANE + hd, cq_ref[...], 0.0), axis=1, keepdims=True)
        m_ref[...] = jnp.full_like(m_ref, -jnp.inf)
        acc_ref[...] = jnp.zeros_like(acc_ref)

    def step(diag_block):
        R = FOX_ROW_CHUNK
        n_chunks = FOX_TQ // R
        masked = diag_block is not None
        key0 = diag_block * FOX_TQ if masked else 0
        v_ones = jnp.concatenate([v_ref[...], jnp.ones(v_ref.shape, BF16)], axis=1)

        def n_keys(rc):
            return key0 + (rc + 1) * R if masked else FOX_TK

        def scores(rc):
            nk = n_keys(rc)
            t = lax.dot_general(qs_ref[rc * R:(rc + 1) * R, :], k_ref[0:nk, :], (((1,), (1,)), ((), ())),
                                preferred_element_type=F32) - ck_ref[:, 0:nk]
            if masked:
                row = key0 + rc * R + lax.broadcasted_iota(jnp.int32, (R, nk), 0)
                col = lax.broadcasted_iota(jnp.int32, (R, nk), 1)
                t = jnp.where(col <= row, t, -jnp.inf)
            return t

        ahead = FOX_SCORES_AHEAD_DIAG if masked else FOX_SCORES_AHEAD
        ts = [scores(c) for c in range(min(ahead, n_chunks))]
        for rc in range(n_chunks):
            rows = slice(rc * R, (rc + 1) * R)
            if rc + ahead < n_chunks:
                ts.append(scores(rc + ahead))
            t = ts[rc]
            cq = cqcol_ref[rows, :]
            m_prev = m_ref[rows, :]
            m_new = jnp.maximum(m_prev, jnp.max(t, axis=1, keepdims=True) + cq)
            alpha = jnp.exp2(m_prev - m_new)
            pr = jnp.exp2(t - (m_new - cq))
            acc_ref[rows, :] = alpha * acc_ref[rows, :] + jnp.dot(
                pr.astype(BF16), v_ones[0:n_keys(rc), :], preferred_element_type=F32)
            m_ref[rows, :] = m_new

    @pl.when(kind == 0)
    def _():
        step(None)

    for b in range(FOX_TK // FOX_TQ):
        @pl.when(kind == 1 + b)
        def _(b=b):
            step(b)
            d = ATT_HEAD_DIM
            o_ref[...] = (acc_ref[:, 0:d] / acc_ref[:, d:2 * d]).astype(o_ref.dtype)


def forgetting_attention(hp, c_cum, c_cum_t):
    s = hp.shape[0]
    tq, tk = FOX_TQ, FOX_TK
    kb = tk // tq
    pairs = [(i, j, 0 if j < i // kb else 1 + i % kb) for i in range(s // tq) for j in range(i // kb + 1)]
    qi, kj, kind = (jnp.asarray(np.array([p[c] for p in pairs], np.int32)) for c in range(3))
    d = ATT_HEAD_DIM
    grid_spec = pltpu.PrefetchScalarGridSpec(
        num_scalar_prefetch=3,
        grid=(ATT_HEADS, len(pairs)),
        in_specs=[pl.BlockSpec((tq, d), lambda h, p, qi, kj, kind: (qi[p], COL_Q // d + h)),
                  pl.BlockSpec((tk, d), lambda h, p, qi, kj, kind: (kj[p], COL_K // d + h)),
                  pl.BlockSpec((tk, d), lambda h, p, qi, kj, kind: (kj[p], COL_V // d + h)),
                  pl.BlockSpec((tq, V7X_LANES), lambda h, p, qi, kj, kind: (qi[p], 0)),
                  pl.BlockSpec((None, 1, tk), lambda h, p, qi, kj, kind: (h, 0, kj[p]))],
        out_specs=pl.BlockSpec((tq, d), lambda h, p, qi, kj, kind: (qi[p], h)),
        scratch_shapes=[pltpu.VMEM((tq, d), BF16),
                        pltpu.VMEM((tq, 1), F32),
                        pltpu.VMEM((tq, 1), F32),
                        pltpu.VMEM((tq, 2 * d), F32)],
    )
    return pl.pallas_call(
        _fox_kernel,
        out_shape=jax.ShapeDtypeStruct((s, ATT_WIDTH), BF16),
        grid_spec=grid_spec,
        compiler_params=_params("parallel", "arbitrary"),
        name="fox_attention",
    )(qi, kj, kind, hp, hp, hp, c_cum, c_cum_t)


def _sg_kernel(u_ref, v_ref, gain_ref, bias_ref, w_ref, bt_ref, o_ref):
    L = SG_CHUNK
    gw = SG_GROUP_WIDTH
    v = v_ref[...].astype(F32)
    mu = jnp.mean(v, axis=-1, keepdims=True)
    xc = v - mu
    var = jnp.mean(xc * xc, axis=-1, keepdims=True)
    vn = (xc * lax.rsqrt(var + NORM_EPS) * gain_ref[...] + bias_ref[...]).astype(BF16)
    ri = lax.broadcasted_iota(jnp.int32, (L, L), 0)
    ci = lax.broadcasted_iota(jnp.int32, (L, L), 1)
    causal = ri >= ci
    for g in range(SG_GROUPS):
        wg = jnp.where(causal, w_ref[g], 0.0).astype(BF16)
        bcol = bt_ref[:, g:g + 1]
        cols = slice(g * gw, (g + 1) * gw)
        for c in range(v_ref.shape[0] // L):
            rows = slice(c * L, (c + 1) * L)
            vm = jnp.dot(wg, vn[rows, cols], preferred_element_type=F32) + bcol
            o_ref[rows, cols] = (u_ref[rows, cols].astype(F32) * vm).astype(o_ref.dtype)


def spatial_gate(hg, ln_gain, ln_bias, w_s, b_s):
    s = hg.shape[0]
    rows = SG_ROWS
    w = SG_WIDTH
    return pl.pallas_call(
        _sg_kernel,
        out_shape=jax.ShapeDtypeStruct((s, w), BF16),
        grid=(s // rows,),
        in_specs=[pl.BlockSpec((rows, w), lambda i: (i, 0)),
                  pl.BlockSpec((rows, w), lambda i: (i, 1)),
                  pl.BlockSpec((1, w), lambda i: (0, 0)),
                  pl.BlockSpec((1, w), lambda i: (0, 0)),
                  pl.BlockSpec((SG_GROUPS, SG_CHUNK, SG_CHUNK), lambda i: (0, 0, 0)),
                  pl.BlockSpec((SG_CHUNK, SG_GROUPS), lambda i: (0, 0))],
        out_specs=pl.BlockSpec((rows, w), lambda i: (i, 0)),
        compiler_params=_params("parallel"),
        name="spatial_gate",
    )(hg, hg, ln_gain.reshape(1, w), ln_bias.reshape(1, w), w_s, b_s.T)


def _even_weights(w_in):
    dt0 = SSD_WIDTH + XBC_WIDTH
    f0 = dt0 + SSD_HEADS + 3 * ATT_WIDTH
    w_ssd = w_in[..., :dt0].astype(BF16)
    w_att = w_in[..., dt0 + SSD_HEADS:f0].astype(BF16)
    w_side = jnp.concatenate([w_in[..., dt0:dt0 + SSD_HEADS], w_in[..., f0:]], axis=-1)
    w_side = jnp.pad(w_side, ((0, 0), (0, 0), (0, SIDE_WIDTH - SSD_HEADS - ATT_HEADS))).astype(BF16)
    return w_ssd, w_att, w_side


def _pad_heads_per_group(v):
    v = v.reshape(SSD_GROUPS, SSD_HEADS_PER_GROUP)
    v = jnp.pad(v, ((0, 0), (0, V7X_LANES - SSD_HEADS_PER_GROUP)))
    return v.reshape(1, SSD_GROUPS * V7X_LANES)


def _even_mixer(xn, i, w_ssd, w_att, w_side, w_out, conv_w, conv_b, dt_bias, a_log, d_skip, norm_gain, f_bias, cast):
    hp_ssd, cast_out = matmul(xn, w_ssd, i, name="even_in_ssd", cast=cast)
    hp_att = matmul(xn, w_att, i, name="even_in_att")
    side = matmul(xn, w_side, i, out_dtype=F32, name="even_side")
    y_ssd = ssd_mixer(hp_ssd, side, conv_w, conv_b.reshape(1, -1),
                      _pad_heads_per_group(dt_bias), _pad_heads_per_group(a_log),
                      jnp.repeat(d_skip, SSD_HEAD_DIM).reshape(1, -1), norm_gain.reshape(1, -1))
    f_bias_pad = jnp.pad(f_bias, (SIDE_F_LANE, V7X_LANES - SIDE_F_LANE - ATT_HEADS)).reshape(1, V7X_LANES)
    c_cum = fox_prep(side, f_bias_pad)
    c_cum_t = c_cum[:, SIDE_F_LANE:SIDE_F_LANE + ATT_HEADS].T.reshape(ATT_HEADS, 1, -1)
    y_att = forgetting_attention(hp_att, c_cum, c_cum_t)
    return matmul_concat2(y_ssd, y_att, w_out, i, name="even_out"), cast_out


def _odd_mixer(xn, i, w_in, w_out, ln_gain, ln_bias, w_s, b_s):
    hg = matmul(xn, w_in, 0, act="gelu_tanh", name="odd_in")
    gated = spatial_gate(hg, ln_gain, ln_bias, w_s, b_s)
    return matmul(gated, w_out, i, name="odd_out")


def kernel(x, pre_mix_gain, post_mix_gain, pre_ffn_gain, post_ffn_gain, even_w_in, even_w_out, ssd_conv_w, ssd_conv_b, ssd_dt_bias, ssd_a_log, ssd_d, ssd_norm_gain, fox_f_bias, odd_w_in, odd_w_out, sg_ln_gain, sg_ln_bias, sg_w_s, sg_b_s, ffn_w_up, ffn_w_down):
    bsz, s, d = x.shape
    assert bsz == 1 and s == SEQ and d == D_MODEL
    h = x.reshape(s, d)
    even_ssd, even_att, even_side = _even_weights(even_w_in)
    even_out, odd_out = even_w_out.astype(BF16), odd_w_out.astype(BF16)
    ffn_up = ffn_w_up[0:1].astype(BF16)
    xn = prenorm(h, pre_mix_gain[0])
    odd_in = None
    for layer in range(DEPTH):
        i = layer // 2
        if layer % 2 == 0:
            y, odd_in = _even_mixer(xn, i, even_ssd, even_att, even_side, even_out, ssd_conv_w[i], ssd_conv_b[i],
                                    ssd_dt_bias[i], ssd_a_log[i], ssd_d[i], ssd_norm_gain[i], fox_f_bias[i],
                                    cast=(odd_w_in, i))
        else:
            y = _odd_mixer(xn, i, odd_in, odd_out, sg_ln_gain[i], sg_ln_bias[i], sg_w_s[i], sg_b_s[i])
        h, xn = resid_norm(h, y, post_mix_gain[layer], pre_ffn_gain[layer])
        hid, ffn_down = matmul(xn, ffn_up, 0, act="relu2", name="ffn_up", cast=(ffn_w_down, layer))
        if layer + 1 < DEPTH:
            y, ffn_up = matmul(hid, ffn_down, 0, name="ffn_down", cast=(ffn_w_up, layer + 1))
        else:
            y = matmul(hid, ffn_down, 0, name="ffn_down")
        next_gain = pre_mix_gain[layer + 1] if layer + 1 < DEPTH else None
        h, xn = resid_norm(h, y, post_ffn_gain[layer], next_gain)
    return h.reshape(bsz, s, d)
```
